```python
import math
import jax, jax.numpy as jnp
from jax import lax
import numpy as np

D_MODEL = 1024
BATCH = 8
SEQ = 4096
DEPTH = 1
DEC_BATCH = 8
DEC_SEQ = 16
PAST_LEN = 1024

CHUNK = 64
HEAD_DIM = 64
N_HEADS_A = D_MODEL // HEAD_DIM
WIDTH_A = N_HEADS_A * HEAD_DIM
DECAY_LORA = 64
ICLR_LORA = 64
WIDTH_B = D_MODEL
CONV_W = 3
PLE_DIM = 256
EPS = 1e-6
GN_EPS = 64e-5
DECAY_SCALE = math.exp(-0.5)
SHIFT_COLS = 3 * WIDTH_A + DECAY_LORA + ICLR_LORA + WIDTH_A
CONV_COLS = 4 * WIDTH_B
GATE_COLS = 2 * D_MODEL
IN_COLS = SHIFT_COLS + CONV_COLS + GATE_COLS

kernel_name = "rwkv7_shortconv_gated_merge_stream_step"


def rmsnorm(x, g):
    xf = x.astype(jnp.float32)
    y = xf * lax.rsqrt(jnp.mean(xf * xf, axis=-1, keepdims=True) + EPS)
    return (y * g.astype(jnp.float32)).astype(x.dtype)


def wkv7_scan(S0, r, w, kh, a, v, kt):
    def step(S, inp):
        r_t, w_t, kh_t, a_t, v_t, kt_t = inp
        sa = jnp.einsum('bhvk,bhk->bhv', S, kh_t)
        S = (S * w_t[:, :, None, :]
             - sa[..., None] * (a_t * kh_t)[:, :, None, :]
             + v_t[..., None] * kt_t[:, :, None, :])
        y_t = jnp.einsum('bhvk,bhk->bhv', S, r_t)
        return S, y_t
    xs = (jnp.moveaxis(r, 1, 0), jnp.moveaxis(w, 1, 0), jnp.moveaxis(kh, 1, 0),
          jnp.moveaxis(a, 1, 0), jnp.moveaxis(v, 1, 0), jnp.moveaxis(kt, 1, 0))
    S, ys = lax.scan(step, S0, xs)
    return jnp.moveaxis(ys, 0, 1), S


def hybrid_layer(x, p, wkv0, shift0, conv0, g_norm, w_in, mu_shift, w_decay0, w_decay2,
                 w_iclr0, w_iclr2, k_removal, k_replace, r_bonus, gn_w, gn_b, conv_w,
                 w_o_a, w_o_b, w_out, g_ple, w_ple_gate, w_ple):
    B_, T, _ = x.shape
    f32 = jnp.float32
    h = rmsnorm(x, g_norm)
    proj = h @ w_in
    pa = proj[..., :SHIFT_COLS]
    pb = proj[..., SHIFT_COLS:SHIFT_COLS + CONV_COLS]
    pg = proj[..., SHIFT_COLS + CONV_COLS:]

    pa_prev = jnp.concatenate([shift0[:, None, :].astype(pa.dtype), pa[:, :-1]], axis=1)
    pa_mix = pa + mu_shift * (pa_prev - pa)
    new_shift = pa[:, -1]
    r, k, v, w_lr, a_lr, z_a = jnp.split(
        pa_mix, [WIDTH_A, 2 * WIDTH_A, 3 * WIDTH_A, 3 * WIDTH_A + DECAY_LORA,
                 3 * WIDTH_A + DECAY_LORA + ICLR_LORA], axis=-1)
    r, k, v = r.astype(f32), k.astype(f32), v.astype(f32)
    d = w_decay0.astype(f32) + jnp.tanh(w_lr.astype(f32)) @ w_decay2.astype(f32)
    w = jnp.exp(-DECAY_SCALE * jax.nn.sigmoid(d))
    a = jax.nn.sigmoid(w_iclr0.astype(f32) + a_lr.astype(f32) @ w_iclr2.astype(f32))
    kappa = k * k_removal.astype(f32)
    kt = k * (1.0 + (a - 1.0) * k_replace.astype(f32))
    hs = (B_, T, N_HEADS_A, HEAD_DIM)
    r_h, w_h, a_h, v_h, kt_h = (t.reshape(hs) for t in (r, w, a, v, kt))
    kap_h = kappa.reshape(hs)
    kh_h = kap_h / jnp.maximum(jnp.linalg.norm(kap_h, axis=-1, keepdims=True), 1e-12)
    y, S = wkv7_scan(wkv0.astype(f32), r_h, w_h, kh_h, a_h, v_h, kt_h)
    mu = jnp.mean(y, axis=-1, keepdims=True)
    var = jnp.mean(jnp.square(y - mu), axis=-1, keepdims=True)
    yn = ((y - mu) * lax.rsqrt(var + GN_EPS)).reshape(B_, T, WIDTH_A)
    yn = yn * gn_w.astype(f32) + gn_b.astype(f32)
    bonus = jnp.sum(r_h * kt_h * r_bonus.astype(f32), axis=-1, keepdims=True) * v_h
    o_a = (yn + bonus.reshape(B_, T, WIDTH_A)).astype(x.dtype) * jax.nn.silu(z_a)
    out_a = o_a @ w_o_a

    gb, gc, xb, z_b = jnp.split(pb, 4, axis=-1)
    u = gc * xb
    u_pad = jnp.concatenate([conv0.astype(u.dtype), u], axis=1)
    cv = sum(conv_w[j] * u_pad[:, j:j + T] for j in range(CONV_W))
    new_conv = u_pad[:, -(CONV_W - 1):]
    out_b = (gb * cv * jax.nn.silu(z_b)) @ w_o_b

    g_a, g_b = jnp.split(jax.nn.sigmoid(pg), 2, axis=-1)
    x = x + (g_a * out_a + g_b * out_b) @ w_out

    x = x + jax.nn.sigmoid(rmsnorm(x, g_ple) @ w_ple_gate) * (p @ w_ple)
    return x, S.astype(x.dtype), new_shift, new_conv


def setup_inputs(seed: int = 0) -> dict:
    key = jax.random.key(seed)
    ks = jax.random.split(key, 32)
    nrm = lambda k, s, sc=1.0: jax.random.normal(k, s, jnp.float32) * sc
    L = DEPTH
    return {
        "x_prompt": nrm(ks[0], (BATCH, SEQ, D_MODEL)),
        "x_sample": nrm(ks[1], (DEC_BATCH, DEC_SEQ, D_MODEL)),
        "state_wkv": nrm(ks[2], (L, DEC_BATCH, N_HEADS_A, HEAD_DIM, HEAD_DIM), 0.5),
        "state_shift": nrm(ks[3], (L, DEC_BATCH, SHIFT_COLS)),
        "state_conv": nrm(ks[4], (L, DEC_BATCH, CONV_W - 1, WIDTH_B)),
        "p_prompt": nrm(ks[5], (L, BATCH, SEQ, PLE_DIM)),
        "p_sample": nrm(ks[6], (L, DEC_BATCH, DEC_SEQ, PLE_DIM)),
        "g_norm": 1.0 + nrm(ks[7], (L, D_MODEL), 0.02),
        "w_in": nrm(ks[8], (L, D_MODEL, IN_COLS), D_MODEL ** -0.5),
        "mu_shift": jax.random.uniform(ks[9], (L, SHIFT_COLS), jnp.float32),
        "w_decay0": jax.random.uniform(ks[10], (L, WIDTH_A), jnp.float32, -2.0, 2.0),
        "w_decay2": nrm(ks[11], (L, DECAY_LORA, WIDTH_A), 0.3 * DECAY_LORA ** -0.5),
        "w_iclr0": nrm(ks[12], (L, WIDTH_A), 0.1),
        "w_iclr2": nrm(ks[13], (L, ICLR_LORA, WIDTH_A), 0.3 * ICLR_LORA ** -0.5),
        "k_removal": 0.85 + nrm(ks[14], (L, WIDTH_A), 0.02),
        "k_replace": 1.0 + nrm(ks[15], (L, WIDTH_A), 0.02),
        "r_bonus": nrm(ks[16], (L, N_HEADS_A, HEAD_DIM), 0.1),
        "gn_w": 1.0 + nrm(ks[17], (L, WIDTH_A), 0.02),
        "gn_b": nrm(ks[18], (L, WIDTH_A), 0.01),
        "conv_w": nrm(ks[19], (L, CONV_W, WIDTH_B), CONV_W ** -0.5),
        "w_o_a": nrm(ks[20], (L, WIDTH_A, D_MODEL), WIDTH_A ** -0.5),
        "w_o_b": nrm(ks[21], (L, WIDTH_B, D_MODEL), WIDTH_B ** -0.5),
        "w_out": nrm(ks[22], (L, D_MODEL, D_MODEL), D_MODEL ** -0.5),
        "g_ple": 1.0 + nrm(ks[23], (L, D_MODEL), 0.02),
        "w_ple_gate": nrm(ks[24], (L, D_MODEL, D_MODEL), D_MODEL ** -0.5),
        "w_ple": nrm(ks[25], (L, PLE_DIM, D_MODEL), PLE_DIM ** -0.5),
        "g_final": 1.0 + nrm(ks[26], (D_MODEL,), 0.02),
    }


def reference(x_prompt, x_sample, state_wkv, state_shift, state_conv, p_prompt, p_sample,
              g_norm, w_in, mu_shift, w_decay0, w_decay2, w_iclr0, w_iclr2, k_removal,
              k_replace, r_bonus, gn_w, gn_b, conv_w, w_o_a, w_o_b, w_out, g_ple,
              w_ple_gate, w_ple, g_final):
    Bp = x_prompt.shape[0]
    xp, xs = x_prompt, x_sample
    wkv_p, shift_p, conv_p, wkv_s, shift_s, conv_s = [], [], [], [], [], []
    for i in range(DEPTH):
        params = (g_norm[i], w_in[i], mu_shift[i], w_decay0[i], w_decay2[i], w_iclr0[i],
                  w_iclr2[i], k_removal[i], k_replace[i], r_bonus[i], gn_w[i], gn_b[i],
                  conv_w[i], w_o_a[i], w_o_b[i], w_out[i], g_ple[i], w_ple_gate[i], w_ple[i])
        wkv0 = jnp.zeros((Bp, N_HEADS_A, HEAD_DIM, HEAD_DIM), jnp.float32)
        shift0 = jnp.zeros((Bp, SHIFT_COLS), xp.dtype)
        conv0 = jnp.zeros((Bp, CONV_W - 1, WIDTH_B), xp.dtype)
        xp, s1, s2, s3 = hybrid_layer(xp, p_prompt[i], wkv0, shift0, conv0, *params)
        wkv_p.append(s1); shift_p.append(s2); conv_p.append(s3)
        xs, t1, t2, t3 = hybrid_layer(xs, p_sample[i], state_wkv[i], state_shift[i],
                                      state_conv[i], *params)
        wkv_s.append(t1); shift_s.append(t2); conv_s.append(t3)
    y_prompt = rmsnorm(xp, g_final)
    y_sample = rmsnorm(xs, g_final)
    return (y_prompt, y_sample,
            jnp.stack(wkv_p), jnp.stack(shift_p), jnp.stack(conv_p),
            jnp.stack(wkv_s), jnp.stack(shift_s), jnp.stack(conv_s))
```

```python
import functools
import math

import jax
import jax.numpy as jnp
from jax import lax
from jax.experimental import pallas as pl
from jax.experimental.pallas import tpu as pltpu

D_MODEL = 1024
HEAD_DIM = 64
N_HEADS = 16
WIDTH_A = N_HEADS * HEAD_DIM
LORA = 64
WIDTH_B = 1024
PLE_DIM = 256
CONV_W = 3
EPS = 1e-6
GN_EPS = 64e-5
DECAY_SCALE = math.exp(-0.5)
SHIFT_COLS = 3 * WIDTH_A + 2 * LORA + WIDTH_A
CONV_COLS = 4 * WIDTH_B
GATE_COLS = 2 * D_MODEL
IN_COLS = SHIFT_COLS + CONV_COLS + GATE_COLS

LANES_V7X = 128
MXU_TILE_V7X = 256
CHUNK = 64
GROUP_W = MXU_TILE_V7X
HEADS_PER_GROUP = GROUP_W // HEAD_DIM
N_GROUPS = WIDTH_A // GROUP_W
LR_COL = 3 * WIDTH_A
ZA_COL = LR_COL + 2 * LORA
PAD_TOP = 8
VMEM_LIMIT_BYTES = 58 * 1024 * 1024

_F32 = jnp.float32
_BF16 = jnp.bfloat16


def _sigmoid(x):
    return 0.5 * jnp.tanh(0.5 * x) + 0.5


def _dot(a, b):
    return jnp.dot(a, b, preferred_element_type=_F32)


def _dot_nt(a, b):
    return lax.dot_general(a, b, (((1,), (1,)), ((), ())), preferred_element_type=_F32)


def _dot_tn(a, b):
    return lax.dot_general(a, b, (((0,), (0,)), ((), ())), preferred_element_type=_F32)


def _rmsnorm(x, g):
    return x * lax.rsqrt(jnp.mean(x * x, axis=-1, keepdims=True) + EPS) * g


def _proj_kernel(x_ref, g_ref, w_ref, o_ref, h_ref):
    @pl.when(pl.program_id(1) == 0)
    def _():
        h_ref[...] = _rmsnorm(x_ref[...], g_ref[...]).astype(_BF16)

    o_ref[...] = _dot(h_ref[...], w_ref[...])


def _in_proj(x2d, g_norm, w_in_bf16, tm, tn):
    m = x2d.shape[0]
    return pl.pallas_call(
        _proj_kernel,
        grid=(m // tm, IN_COLS // tn),
        in_specs=[
            pl.BlockSpec((tm, D_MODEL), lambda i, j: (i, 0)),
            pl.BlockSpec((1, D_MODEL), lambda i, j: (0, 0)),
            pl.BlockSpec((D_MODEL, tn), lambda i, j: (0, j)),
        ],
        out_specs=pl.BlockSpec((tm, tn), lambda i, j: (i, j)),
        out_shape=jax.ShapeDtypeStruct((m, IN_COLS), _F32),
        scratch_shapes=[pltpu.VMEM((tm, D_MODEL), _BF16)],
        compiler_params=pltpu.CompilerParams(
            dimension_semantics=("arbitrary", "arbitrary"),
            vmem_limit_bytes=VMEM_LIMIT_BYTES),
        name="in_proj",
    )(x2d, g_norm, w_in_bf16)


def _block_diag_stack(x, head_of_lane):
    parts = [jnp.where(head_of_lane == h, x, 0.0) for h in range(HEADS_PER_GROUP)]
    return jnp.concatenate(parts, axis=0)


def _scan_unit(r, lw, kh, p, v, kt, s0):
    c = CHUNK
    row = lax.broadcasted_iota(jnp.int32, (c, GROUP_W), 0)
    lane = lax.broadcasted_iota(jnp.int32, (c, GROUP_W), 1)
    pos = lane & (HEAD_DIM - 1)
    head = lane >> 6
    strict = pos < row
    incl = pos <= row
    eye = pos == row
    r2 = lax.broadcasted_iota(jnp.int32, (GROUP_W, GROUP_W), 0)
    l2 = lax.broadcasted_iota(jnp.int32, (GROUP_W, GROUP_W), 1)
    same_head = (r2 >> 6) == (l2 >> 6)

    tr = lax.broadcasted_iota(jnp.int32, (c, c), 0)
    tc_ = lax.broadcasted_iota(jnp.int32, (c, c), 1)
    tri = (tc_ <= tr).astype(_BF16)
    hi = lw.astype(_BF16)
    rem = lw - hi.astype(_F32)
    mid = rem.astype(_BF16)
    lo = (rem - mid.astype(_F32)).astype(_BF16)
    cl3 = _dot(tri, jnp.concatenate([hi, mid, lo], axis=1))
    cl = cl3[:, :GROUP_W] + cl3[:, GROUP_W:2 * GROUP_W] + cl3[:, 2 * GROUP_W:]
    cl_end = cl[c - 1:c, :]

    e_in = jnp.exp(cl)
    e_ex = jnp.exp(cl - lw)
    e_inv = jnp.exp(-cl)
    g_end = jnp.exp(cl_end)
    rd = r * e_in
    khd = kh * e_ex
    pd = p * e_inv
    ktd = kt * e_inv
    pdg = pd * g_end
    ktdg = ktd * g_end

    bd = functools.partial(_block_diag_stack, head_of_lane=head)

    lhs = jnp.concatenate([khd, rd], axis=0).astype(_BF16)
    rhs = jnp.concatenate([bd(pd), bd(ktd)], axis=0).astype(_BF16)
    ab = _dot_nt(lhs, rhs)
    a_all = jnp.where(strict, ab[:c, :GROUP_W], 0.0)
    b_all = jnp.where(strict, ab[:c, GROUP_W:], 0.0)
    ay_all = jnp.where(incl, ab[c:, :GROUP_W], 0.0)
    by_all = jnp.where(incl, ab[c:, GROUP_W:], 0.0)

    n = -a_all
    t_all = jnp.where(eye, 1.0, 0.0) + n
    n = _dot(n.astype(_BF16), bd(n).astype(_BF16))
    for step in range(5):
        if step < 4:
            both = _dot(jnp.concatenate([t_all, n], axis=0).astype(_BF16), bd(n).astype(_BF16))
            t_all = t_all + both[:c]
            n = both[c:]
        else:
            t_all = t_all + _dot(t_all.astype(_BF16), bd(n).astype(_BF16))

    bdv = bd(v).astype(_BF16)
    bv2 = _dot(jnp.concatenate([b_all, by_all], axis=0).astype(_BF16), bdv)
    bv = bv2[:c]
    byv = bv2[c:]

    tz = _dot(t_all.astype(_BF16),
              jnp.concatenate([bd(khd), bd(bv)], axis=1).astype(_BF16))
    wt = tz[:, :GROUP_W]
    ut = tz[:, GROUP_W:]
    aywu = _dot(ay_all.astype(_BF16),
                jnp.concatenate([bd(wt), bd(ut)], axis=1).astype(_BF16))
    qt = rd - aywu[:, :GROUP_W]
    y0 = byv - aywu[:, GROUP_W:]

    gmat = jnp.where(same_head, _dot_tn(pdg.astype(_BF16), wt.astype(_BF16)), 0.0)
    nt = jnp.where(
        same_head,
        _dot_tn(jnp.concatenate([v, ut], axis=0).astype(_BF16),
                jnp.concatenate([ktdg, -pdg], axis=0).astype(_BF16)),
        0.0)

    s0b = s0.astype(_BF16)
    y = _dot_nt(qt.astype(_BF16), s0b) + y0
    s1 = s0 * g_end - _dot_nt(s0b, gmat.astype(_BF16)) + nt
    return y, s1


def _mixer_kernel(proj_ref, x_ref, pe_ref, s0_ref, shift0_ref, conv0_ref,
                  mu_ref, wd0_ref, wi0_ref, krem_ref, krep_ref, rb_ref, gnw_ref, gnb_ref,
                  cw_ref, gple_ref, gfin_ref,
                  wlora_ref, woa_ref, wob_ref, wout_ref, wpg_ref, wple_ref,
                  y_ref, sout_ref, shout_ref, cvout_ref,
                  pbuf, ubuf, oabuf, s_ref, *, tc, n_valid):
    j = pl.program_id(1)
    last = pl.num_programs(1) - 1
    c = CHUNK

    @pl.when(j == 0)
    def _():
        s_ref[...] = s0_ref[...]
        pbuf[PAD_TOP - 1:PAD_TOP, :] = shift0_ref[...]
        ubuf[PAD_TOP - 2:PAD_TOP, :] = conv0_ref[...]

    pbuf[PAD_TOP:PAD_TOP + tc, :] = proj_ref[:, :SHIFT_COLS]

    r2 = lax.broadcasted_iota(jnp.int32, (GROUP_W, GROUP_W), 0)
    l2 = lax.broadcasted_iota(jnp.int32, (GROUP_W, GROUP_W), 1)
    ones_bd = ((r2 >> 6) == (l2 >> 6)).astype(_BF16)

    def head_sum(x):
        return _dot(x.astype(_BF16), ones_bd)

    def chunk_body(ci):
        r0 = ci * c

        def mixed(col0, width):
            cur = pbuf[PAD_TOP + r0:PAD_TOP + r0 + c, col0:col0 + width]
            prv = pbuf[PAD_TOP - 1 + r0:PAD_TOP - 1 + r0 + c, col0:col0 + width]
            return cur + mu_ref[:, col0:col0 + width] * (prv - cur)

        lr = mixed(LR_COL, 2 * LORA)
        lr_lane = lax.broadcasted_iota(jnp.int32, lr.shape, 1)
        lr_act = jnp.where(lr_lane < LORA, jnp.tanh(lr), lr).astype(_BF16)
        if n_valid < tc:
            valid = (lax.broadcasted_iota(jnp.int32, (c, GROUP_W), 0) + r0) < n_valid

        for g in range(N_GROUPS):
            c0 = g * GROUP_W
            cs = slice(c0, c0 + GROUP_W)
            r = mixed(c0, GROUP_W)
            k = mixed(WIDTH_A + c0, GROUP_W)
            v = mixed(2 * WIDTH_A + c0, GROUP_W)
            za = mixed(ZA_COL + c0, GROUP_W)

            d = wd0_ref[:, cs] + _dot(lr_act, wlora_ref[:, cs])
            lw = -DECAY_SCALE * _sigmoid(d)
            a = _sigmoid(wi0_ref[:, cs] + _dot(lr_act, wlora_ref[:, WIDTH_A + c0:WIDTH_A + c0 + GROUP_W]))
            kappa = k * krem_ref[:, cs]
            n2 = head_sum(kappa * kappa)
            kh = kappa * lax.rsqrt(jnp.maximum(n2, 1e-24))
            kt = k * (1.0 + (a - 1.0) * krep_ref[:, cs])
            p = a * kh
            bonus = head_sum(r * kt * rb_ref[:, cs]) * v
            if n_valid < tc:
                lw = jnp.where(valid, lw, 0.0)
                p = jnp.where(valid, p, 0.0)
                kt_s = jnp.where(valid, kt, 0.0)
            else:
                kt_s = kt

            y, s1 = _scan_unit(r, lw, kh, p, v, kt_s, s_ref[g])
            s_ref[g] = s1

            mean = head_sum(y) * (1.0 / HEAD_DIM)
            yc = y - mean
            var = head_sum(yc * yc) * (1.0 / HEAD_DIM)
            yn = yc * lax.rsqrt(var + GN_EPS) * gnw_ref[:, cs] + gnb_ref[:, cs]
            oa = (yn + bonus) * (za * _sigmoid(za))
            oabuf[r0:r0 + c, cs] = oa.astype(_BF16)

    for ci in range(tc // c):
        chunk_body(ci)

    pbuf[PAD_TOP - 1:PAD_TOP, :] = pbuf[PAD_TOP + tc - 1:PAD_TOP + tc, :]

    b0 = SHIFT_COLS
    gb = proj_ref[:, b0:b0 + WIDTH_B]
    gc = proj_ref[:, b0 + WIDTH_B:b0 + 2 * WIDTH_B]
    xb = proj_ref[:, b0 + 2 * WIDTH_B:b0 + 3 * WIDTH_B]
    zb = proj_ref[:, b0 + 3 * WIDTH_B:b0 + 4 * WIDTH_B]
    u = gc * xb
    ubuf[PAD_TOP:PAD_TOP + tc, :] = u
    cv = (cw_ref[0:1, :] * ubuf[PAD_TOP - 2:PAD_TOP - 2 + tc, :]
          + cw_ref[1:2, :] * ubuf[PAD_TOP - 1:PAD_TOP - 1 + tc, :]
          + cw_ref[2:3, :] * u)
    ob = (gb * cv * (zb * _sigmoid(zb))).astype(_BF16)
    out_b = _dot(ob, wob_ref[...])
    out_a = _dot(oabuf[...], woa_ref[...])

    @pl.when(j == last)
    def _():
        sout_ref[...] = s_ref[...]
        shout_ref[...] = pbuf[PAD_TOP + n_valid - 1:PAD_TOP + n_valid, :]
        cvout_ref[...] = ubuf[PAD_TOP + n_valid - 2:PAD_TOP + n_valid, :]

    ubuf[PAD_TOP - 2:PAD_TOP, :] = ubuf[PAD_TOP + tc - 2:PAD_TOP + tc, :]

    g0 = SHIFT_COLS + CONV_COLS
    ga = _sigmoid(proj_ref[:, g0:g0 + D_MODEL])
    gbm = _sigmoid(proj_ref[:, g0 + D_MODEL:g0 + 2 * D_MODEL])
    m = (ga * out_a + gbm * out_b).astype(_BF16)
    x1 = x_ref[...] + _dot(m, wout_ref[...])

    hp = _rmsnorm(x1, gple_ref[...]).astype(_BF16)
    gate = _sigmoid(_dot(hp, wpg_ref[...]))
    x2 = x1 + gate * _dot(pe_ref[...].astype(_BF16), wple_ref[...])
    y_ref[...] = _rmsnorm(x2, gfin_ref[...])


def _mixer(proj, x, pe, s0_bd, shift0, conv0, vecs, mats, tc, n_valid):
    b, t, _ = x.shape
    nt = t // tc
    assert n_valid == tc or nt == 1

    def tile(width):
        return pl.BlockSpec((None, tc, width), lambda bi, ji: (bi, ji, 0))

    def per_batch(shape):
        return pl.BlockSpec((None,) + shape, lambda bi, ji: (bi,) + (0,) * len(shape))

    def whole(arr):
        return pl.BlockSpec(arr.shape, lambda bi, ji: (0,) * arr.ndim)

    state_shape = (N_GROUPS, GROUP_W, GROUP_W)
    kern = functools.partial(_mixer_kernel, tc=tc, n_valid=n_valid)
    return pl.pallas_call(
        kern,
        grid=(b, nt),
        in_specs=[tile(IN_COLS), tile(D_MODEL), tile(PLE_DIM),
                  per_batch(state_shape), per_batch((1, SHIFT_COLS)), per_batch((CONV_W - 1, WIDTH_B))]
                 + [whole(a) for a in vecs] + [whole(a) for a in mats],
        out_specs=[tile(D_MODEL), per_batch(state_shape), per_batch((1, SHIFT_COLS)),
                   per_batch((CONV_W - 1, WIDTH_B))],
        out_shape=[jax.ShapeDtypeStruct((b, t, D_MODEL), _F32),
                   jax.ShapeDtypeStruct((b,) + state_shape, _F32),
                   jax.ShapeDtypeStruct((b, 1, SHIFT_COLS), _F32),
                   jax.ShapeDtypeStruct((b, CONV_W - 1, WIDTH_B), _F32)],
        scratch_shapes=[pltpu.VMEM((PAD_TOP + tc, SHIFT_COLS), _F32),
                        pltpu.VMEM((PAD_TOP + tc, WIDTH_B), _F32),
                        pltpu.VMEM((tc, WIDTH_A), _BF16),
                        pltpu.VMEM(state_shape, _F32)],
        compiler_params=pltpu.CompilerParams(
            dimension_semantics=("arbitrary", "arbitrary"),
            vmem_limit_bytes=VMEM_LIMIT_BYTES),
        name="mixer",
    )(proj, x, pe, s0_bd, shift0, conv0, *vecs, *mats)


def _to_block_diag(wkv):
    b = wkv.shape[0]
    w = wkv.reshape(b, N_GROUPS, HEADS_PER_GROUP, HEAD_DIM, HEAD_DIM)
    eye = jnp.eye(HEADS_PER_GROUP, dtype=wkv.dtype)
    out = w[:, :, :, :, None, :] * eye[None, None, :, None, :, None]
    return out.reshape(b, N_GROUPS, GROUP_W, GROUP_W)


def _from_block_diag(s_bd):
    b = s_bd.shape[0]
    w = s_bd.reshape(b, N_GROUPS, HEADS_PER_GROUP, HEAD_DIM, HEADS_PER_GROUP, HEAD_DIM)
    idx = jnp.arange(HEADS_PER_GROUP)
    diag = w[:, :, idx, :, idx, :]
    diag = jnp.moveaxis(diag, 0, 2)
    return diag.reshape(b, N_HEADS, HEAD_DIM, HEAD_DIM)


def _layer(x, pe, wkv0, shift0, conv0, g_norm, w_in_bf16, vecs, mats, tc, proj_tm, proj_tn):
    b, t, _ = x.shape
    n_valid = tc
    if t < tc:
        n_valid = t
        x = jnp.pad(x, ((0, 0), (0, tc - t), (0, 0)))
        pe = jnp.pad(pe, ((0, 0), (0, tc - t), (0, 0)))
    tp = x.shape[1]
    proj = _in_proj(x.reshape(b * tp, D_MODEL), g_norm, w_in_bf16, proj_tm, proj_tn)
    proj = proj.reshape(b, tp, IN_COLS)
    y, s_bd, sh, cvs = _mixer(proj, x, pe, _to_block_diag(wkv0.astype(_F32)),
                              shift0.reshape(b, 1, SHIFT_COLS), conv0, vecs, mats, tc, n_valid)
    return y[:, :t], _from_block_diag(s_bd), sh.reshape(b, SHIFT_COLS), cvs


def kernel(x_prompt, x_sample, state_wkv, state_shift, state_conv, p_prompt, p_sample, g_norm, w_in, mu_shift, w_decay0, w_decay2, w_iclr0, w_iclr2, k_removal, k_replace, r_bonus, gn_w, gn_b, conv_w, w_o_a, w_o_b, w_out, g_ple, w_ple_gate, w_ple, g_final):
    assert g_norm.shape[0] == 1, "single-layer trunk"
    bp = x_prompt.shape[0]
    row = lambda a: a.reshape(1, -1).astype(_F32)
    zeros = jnp.zeros((LORA, WIDTH_A), _F32)
    w_lora = jnp.concatenate(
        [jnp.concatenate([w_decay2[0], zeros], axis=1),
         jnp.concatenate([zeros, w_iclr2[0]], axis=1)], axis=0).astype(_BF16)
    vecs = [row(mu_shift[0]), row(w_decay0[0]), row(w_iclr0[0]), row(k_removal[0]), row(k_replace[0]),
            row(r_bonus[0]), row(gn_w[0]), row(gn_b[0]), conv_w[0].astype(_F32), row(g_ple[0]), row(g_final)]
    mats = [w_lora, w_o_a[0].astype(_BF16), w_o_b[0].astype(_BF16), w_out[0].astype(_BF16),
            w_ple_gate[0].astype(_BF16), w_ple[0].astype(_BF16)]
    w_in_bf16 = w_in[0].astype(_BF16)
    g_n = row(g_norm[0])

    zero_wkv = jnp.zeros((bp, N_HEADS, HEAD_DIM, HEAD_DIM), _F32)
    zero_shift = jnp.zeros((bp, SHIFT_COLS), _F32)
    zero_conv = jnp.zeros((bp, CONV_W - 1, WIDTH_B), _F32)
    yp, wkv_p, shift_p, conv_p = _layer(
        x_prompt, p_prompt[0], zero_wkv, zero_shift, zero_conv, g_n, w_in_bf16, vecs, mats,
        tc=128, proj_tm=1024, proj_tn=1152)
    ys, wkv_s, shift_s, conv_s = _layer(
        x_sample, p_sample[0], state_wkv[0], state_shift[0], state_conv[0], g_n, w_in_bf16, vecs, mats,
        tc=CHUNK, proj_tm=512, proj_tn=1152)
    dt = x_prompt.dtype
    return (yp, ys,
            wkv_p.astype(dt)[None], shift_p[None], conv_p[None],
            wkv_s.astype(dt)[None], shift_s[None], conv_s[None])
```

```python
import functools
import math

import jax
import jax.numpy as jnp
from jax import lax
from jax.experimental import pallas as pl
from jax.experimental.pallas import tpu as pltpu

D_MODEL = 1024
HEAD_DIM = 64
N_HEADS = 16
WIDTH_A = N_HEADS * HEAD_DIM
LORA = 64
WIDTH_B = 1024
PLE_DIM = 256
CONV_W = 3
EPS = 1e-6
GN_EPS = 64e-5
DECAY_SCALE = math.exp(-0.5)
SHIFT_COLS = 3 * WIDTH_A + 2 * LORA + WIDTH_A
CONV_COLS = 4 * WIDTH_B
GATE_COLS = 2 * D_MODEL
IN_COLS = SHIFT_COLS + CONV_COLS + GATE_COLS

LANES_V7X = 128
MXU_TILE_V7X = 256
CHUNK = 64
GROUP_W = MXU_TILE_V7X
HEADS_PER_GROUP = GROUP_W // HEAD_DIM
N_GROUPS = WIDTH_A // GROUP_W
LR_COL = 3 * WIDTH_A
ZA_COL = LR_COL + 2 * LORA
PAD_TOP = 8
VMEM_LIMIT_BYTES = 58 * 1024 * 1024

_F32 = jnp.float32
_BF16 = jnp.bfloat16


def _sigmoid(x):
    return 0.5 * jnp.tanh(0.5 * x) + 0.5


def _dot(a, b):
    return jnp.dot(a, b, preferred_element_type=_F32)


def _dot_nt(a, b):
    return lax.dot_general(a, b, (((1,), (1,)), ((), ())), preferred_element_type=_F32)


def _dot_tn(a, b):
    return lax.dot_general(a, b, (((0,), (0,)), ((), ())), preferred_element_type=_F32)


def _rmsnorm(x, g):
    return x * lax.rsqrt(jnp.mean(x * x, axis=-1, keepdims=True) + EPS) * g


def _proj_kernel(x_ref, g_ref, w_ref, o_ref, h_ref):
    @pl.when(pl.program_id(1) == 0)
    def _():
        h_ref[...] = _rmsnorm(x_ref[...], g_ref[...]).astype(_BF16)

    o_ref[...] = _dot(h_ref[...], w_ref[...])


def _in_proj(x2d, g_norm, w_in_bf16, tm, tn):
    m = x2d.shape[0]
    return pl.pallas_call(
        _proj_kernel,
        grid=(m // tm, IN_COLS // tn),
        in_specs=[
            pl.BlockSpec((tm, D_MODEL), lambda i, j: (i, 0)),
            pl.BlockSpec((1, D_MODEL), lambda i, j: (0, 0)),
            pl.BlockSpec((D_MODEL, tn), lambda i, j: (0, j)),
        ],
        out_specs=pl.BlockSpec((tm, tn), lambda i, j: (i, j)),
        out_shape=jax.ShapeDtypeStruct((m, IN_COLS), _F32),
        scratch_shapes=[pltpu.VMEM((tm, D_MODEL), _BF16)],
        compiler_params=pltpu.CompilerParams(
            dimension_semantics=("arbitrary", "arbitrary"),
            vmem_limit_bytes=VMEM_LIMIT_BYTES),
        name="in_proj",
    )(x2d, g_norm, w_in_bf16)


def _block_diag_stack(x, head_of_lane):
    parts = [jnp.where(head_of_lane == h, x, 0.0) for h in range(HEADS_PER_GROUP)]
    return jnp.concatenate(parts, axis=0)


def _unit_stages(load, params, wlora_ref, head_sum, s_ref, oabuf, g, r0, valid):
    c = CHUNK
    c0 = g * GROUP_W
    cs = slice(c0, c0 + GROUP_W)
    mixed, lr_act = load
    wd0_ref, wi0_ref, krem_ref, krep_ref, rb_ref, gnw_ref, gnb_ref = params

    row = lax.broadcasted_iota(jnp.int32, (c, GROUP_W), 0)
    lane = lax.broadcasted_iota(jnp.int32, (c, GROUP_W), 1)
    pos = lane & (HEAD_DIM - 1)
    head = lane >> 6
    strict = pos < row
    incl = pos <= row
    eye = pos == row
    r2 = lax.broadcasted_iota(jnp.int32, (GROUP_W, GROUP_W), 0)
    l2 = lax.broadcasted_iota(jnp.int32, (GROUP_W, GROUP_W), 1)
    same_head = (r2 >> 6) == (l2 >> 6)
    bd = functools.partial(_block_diag_stack, head_of_lane=head)

    r = mixed(c0, GROUP_W)
    k = mixed(WIDTH_A + c0, GROUP_W)
    v = mixed(2 * WIDTH_A + c0, GROUP_W)
    d_lora = _dot(lr_act, wlora_ref[:, cs])
    a_lora = _dot(lr_act, wlora_ref[:, WIDTH_A + c0:WIDTH_A + c0 + GROUP_W])
    kappa = k * krem_ref[:, cs]
    n2 = head_sum(kappa * kappa)
    yield

    lw = -DECAY_SCALE * _sigmoid(wd0_ref[:, cs] + d_lora)
    a = _sigmoid(wi0_ref[:, cs] + a_lora)
    kh = kappa * lax.rsqrt(jnp.maximum(n2, 1e-24))
    kt = k * (1.0 + (a - 1.0) * krep_ref[:, cs])
    p = a * kh
    bonus_sum = head_sum(r * kt * rb_ref[:, cs])
    if valid is not None:
        lw = jnp.where(valid, lw, 0.0)
        p = jnp.where(valid, p, 0.0)
        kt = jnp.where(valid, kt, 0.0)

    tr = lax.broadcasted_iota(jnp.int32, (c, c), 0)
    tc_ = lax.broadcasted_iota(jnp.int32, (c, c), 1)
    tri = (tc_ <= tr).astype(_BF16)
    hi = lw.astype(_BF16)
    rem = lw - hi.astype(_F32)
    mid = rem.astype(_BF16)
    lo = (rem - mid.astype(_F32)).astype(_BF16)
    cl3 = _dot(tri, jnp.concatenate([hi, mid, lo], axis=1))
    yield

    cl = cl3[:, :GROUP_W] + cl3[:, GROUP_W:2 * GROUP_W] + cl3[:, 2 * GROUP_W:]
    g_end = jnp.exp(cl[c - 1:c, :])
    e_inv = jnp.exp(-cl)
    rd = r * jnp.exp(cl)
    khd = kh * jnp.exp(cl - lw)
    pd = p * e_inv
    ktd = kt * e_inv
    pdg = pd * g_end
    ktdg = ktd * g_end
    ab = _dot_nt(jnp.concatenate([khd, rd], axis=0).astype(_BF16),
                 jnp.concatenate([bd(pd), bd(ktd)], axis=0).astype(_BF16))
    yield

    b_all = jnp.where(strict, ab[:c, GROUP_W:], 0.0)
    by_all = jnp.where(incl, ab[c:, GROUP_W:], 0.0)
    ay_all = jnp.where(incl, ab[c:, :GROUP_W], 0.0)
    n = jnp.where(strict, -ab[:c, :GROUP_W], 0.0)
    t_all = jnp.where(eye, 1.0, 0.0) + n
    bv2 = _dot(jnp.concatenate([b_all, by_all], axis=0).astype(_BF16), bd(v).astype(_BF16))
    n = _dot(n.astype(_BF16), bd(n).astype(_BF16))
    yield
    for _ in range(4):
        both = _dot(jnp.concatenate([t_all, n], axis=0).astype(_BF16), bd(n).astype(_BF16))
        yield
        t_all = t_all + both[:c]
        n = both[c:]
    tn = _dot(t_all.astype(_BF16), bd(n).astype(_BF16))
    yield
    t_all = t_all + tn

    bv = bv2[:c]
    byv = bv2[c:]
    tz = _dot(t_all.astype(_BF16), jnp.concatenate([bd(khd), bd(bv)], axis=1).astype(_BF16))
    yield
    wt = tz[:, :GROUP_W]
    ut = tz[:, GROUP_W:]
    aywu = _dot(ay_all.astype(_BF16), jnp.concatenate([bd(wt), bd(ut)], axis=1).astype(_BF16))
    gmat = _dot_tn(pdg.astype(_BF16), wt.astype(_BF16))
    nt = _dot_tn(jnp.concatenate([v, ut], axis=0).astype(_BF16),
                 jnp.concatenate([ktdg, -pdg], axis=0).astype(_BF16))
    yield
    qt = (rd - aywu[:, :GROUP_W]).astype(_BF16)
    y0 = byv - aywu[:, GROUP_W:]
    gmat = jnp.where(same_head, gmat, 0.0).astype(_BF16)
    nt = jnp.where(same_head, nt, 0.0)

    s0 = s_ref[g]
    s0b = s0.astype(_BF16)
    y = _dot_nt(qt, s0b) + y0
    s_ref[g] = s0 * g_end - _dot_nt(s0b, gmat) + nt
    yield

    mean = head_sum(y) * (1.0 / HEAD_DIM)
    yield
    yc = y - mean
    var = head_sum(yc * yc) * (1.0 / HEAD_DIM)
    yield
    yn = yc * lax.rsqrt(var + GN_EPS) * gnw_ref[:, cs] + gnb_ref[:, cs]
    za = mixed(ZA_COL + c0, GROUP_W)
    oa = (yn + bonus_sum * v) * (za * _sigmoid(za))
    oabuf[r0:r0 + c, cs] = oa.astype(_BF16)


def _run_lockstep(units):
    units = list(units)
    while units:
        alive = []
        for u in units:
            try:
                next(u)
                alive.append(u)
            except StopIteration:
                pass
        units = alive


def _mixer_kernel(proj_ref, x_ref, pe_ref, s0_ref, shift0_ref, conv0_ref,
                  mu_ref, wd0_ref, wi0_ref, krem_ref, krep_ref, rb_ref, gnw_ref, gnb_ref,
                  cw_ref, gple_ref, gfin_ref,
                  wlora_ref, woa_ref, wob_ref, wout_ref, wpg_ref, wple_ref,
                  y_ref, sout_ref, shout_ref, cvout_ref,
                  pbuf, ubuf, oabuf, s_ref, *, tc, n_valid):
    j = pl.program_id(1)
    last = pl.num_programs(1) - 1
    c = CHUNK

    @pl.when(j == 0)
    def _():
        s_ref[...] = s0_ref[...]
        pbuf[PAD_TOP - 1:PAD_TOP, :] = shift0_ref[...]
        ubuf[PAD_TOP - 2:PAD_TOP, :] = conv0_ref[...]

    pbuf[PAD_TOP:PAD_TOP + tc, :] = proj_ref[:, :SHIFT_COLS]

    r2 = lax.broadcasted_iota(jnp.int32, (GROUP_W, GROUP_W), 0)
    l2 = lax.broadcasted_iota(jnp.int32, (GROUP_W, GROUP_W), 1)
    ones_bd = ((r2 >> 6) == (l2 >> 6)).astype(_BF16)

    def head_sum(x):
        return _dot(x.astype(_BF16), ones_bd)

    params = (wd0_ref, wi0_ref, krem_ref, krep_ref, rb_ref, gnw_ref, gnb_ref)

    def chunk_units(ci):
        r0 = ci * c

        def mixed(col0, width):
            cur = pbuf[PAD_TOP + r0:PAD_TOP + r0 + c, col0:col0 + width]
            prv = pbuf[PAD_TOP - 1 + r0:PAD_TOP - 1 + r0 + c, col0:col0 + width]
            return cur + mu_ref[:, col0:col0 + width] * (prv - cur)

        lr = mixed(LR_COL, 2 * LORA)
        lr_lane = lax.broadcasted_iota(jnp.int32, lr.shape, 1)
        lr_act = jnp.where(lr_lane < LORA, jnp.tanh(lr), lr).astype(_BF16)
        valid = None
        if n_valid < tc:
            valid = (lax.broadcasted_iota(jnp.int32, (c, GROUP_W), 0) + r0) < n_valid
        return [_unit_stages((mixed, lr_act), params, wlora_ref, head_sum, s_ref, oabuf, g, r0, valid)
                for g in range(N_GROUPS)]

    _run_lockstep(u for ci in range(tc // c) for u in chunk_units(ci))

    pbuf[PAD_TOP - 1:PAD_TOP, :] = pbuf[PAD_TOP + tc - 1:PAD_TOP + tc, :]

    b0 = SHIFT_COLS
    gb = proj_ref[:, b0:b0 + WIDTH_B]
    gc = proj_ref[:, b0 + WIDTH_B:b0 + 2 * WIDTH_B]
    xb = proj_ref[:, b0 + 2 * WIDTH_B:b0 + 3 * WIDTH_B]
    zb = proj_ref[:, b0 + 3 * WIDTH_B:b0 + 4 * WIDTH_B]
    u = gc * xb
    ubuf[PAD_TOP:PAD_TOP + tc, :] = u
    cv = (cw_ref[0:1, :] * ubuf[PAD_TOP - 2:PAD_TOP - 2 + tc, :]
          + cw_ref[1:2, :] * ubuf[PAD_TOP - 1:PAD_TOP - 1 + tc, :]
          + cw_ref[2:3, :] * u)
    ob = (gb * cv * (zb * _sigmoid(zb))).astype(_BF16)
    out_b = _dot(ob, wob_ref[...])
    out_a = _dot(oabuf[...], woa_ref[...])

    @pl.when(j == last)
    def _():
        sout_ref[...] = s_ref[...]
        shout_ref[...] = pbuf[PAD_TOP + n_valid - 1:PAD_TOP + n_valid, :]
        cvout_ref[...] = ubuf[PAD_TOP + n_valid - 2:PAD_TOP + n_valid, :]

    ubuf[PAD_TOP - 2:PAD_TOP, :] = ubuf[PAD_TOP + tc - 2:PAD_TOP + tc, :]

    g0 = SHIFT_COLS + CONV_COLS
    ga = _sigmoid(proj_ref[:, g0:g0 + D_MODEL])
    gbm = _sigmoid(proj_ref[:, g0 + D_MODEL:g0 + 2 * D_MODEL])
    m = (ga * out_a + gbm * out_b).astype(_BF16)
    x1 = x_ref[...] + _dot(m, wout_ref[...])

    hp = _rmsnorm(x1, gple_ref[...]).astype(_BF16)
    gate = _sigmoid(_dot(hp, wpg_ref[...]))
    x2 = x1 + gate * _dot(pe_ref[...].astype(_BF16), wple_ref[...])
    y_ref[...] = _rmsnorm(x2, gfin_ref[...])


def _mixer(proj, x, pe, s0_bd, shift0, conv0, vecs, mats, tc, n_valid):
    b, t, _ = x.shape
    nt = t // tc
    assert n_valid == tc or nt == 1

    def tile(width):
        return pl.BlockSpec((None, tc, width), lambda bi, ji: (bi, ji, 0))

    def per_batch(shape):
        return pl.BlockSpec((None,) + shape, lambda bi, ji: (bi,) + (0,) * len(shape))

    def whole(arr):
        return pl.BlockSpec(arr.shape, lambda bi, ji: (0,) * arr.ndim)

    state_shape = (N_GROUPS, GROUP_W, GROUP_W)
    kern = functools.partial(_mixer_kernel, tc=tc, n_valid=n_valid)
    return pl.pallas_call(
        kern,
        grid=(b, nt),
        in_specs=[tile(IN_COLS), tile(D_MODEL), tile(PLE_DIM),
                  per_batch(state_shape), per_batch((1, SHIFT_COLS)), per_batch((CONV_W - 1, WIDTH_B))]
                 + [whole(a) for a in vecs] + [whole(a) for a in mats],
        out_specs=[tile(D_MODEL), per_batch(state_shape), per_batch((1, SHIFT_COLS)),
                   per_batch((CONV_W - 1, WIDTH_B))],
        out_shape=[jax.ShapeDtypeStruct((b, t, D_MODEL), _F32),
                   jax.ShapeDtypeStruct((b,) + state_shape, _F32),
                   jax.ShapeDtypeStruct((b, 1, SHIFT_COLS), _F32),
                   jax.ShapeDtypeStruct((b, CONV_W - 1, WIDTH_B), _F32)],
        scratch_shapes=[pltpu.VMEM((PAD_TOP + tc, SHIFT_COLS), _F32),
                        pltpu.VMEM((PAD_TOP + tc, WIDTH_B), _F32),
                        pltpu.VMEM((tc, WIDTH_A), _BF16),
                        pltpu.VMEM(state_shape, _F32)],
        compiler_params=pltpu.CompilerParams(
            dimension_semantics=("arbitrary", "arbitrary"),
            vmem_limit_bytes=VMEM_LIMIT_BYTES),
        name="mixer",
    )(proj, x, pe, s0_bd, shift0, conv0, *vecs, *mats)


def _to_block_diag(wkv):
    b = wkv.shape[0]
    w = wkv.reshape(b, N_GROUPS, HEADS_PER_GROUP, HEAD_DIM, HEAD_DIM)
    eye = jnp.eye(HEADS_PER_GROUP, dtype=wkv.dtype)
    out = w[:, :, :, :, None, :] * eye[None, None, :, None, :, None]
    return out.reshape(b, N_GROUPS, GROUP_W, GROUP_W)


def _from_block_diag(s_bd):
    b = s_bd.shape[0]
    w = s_bd.reshape(b, N_GROUPS, HEADS_PER_GROUP, HEAD_DIM, HEADS_PER_GROUP, HEAD_DIM)
    idx = jnp.arange(HEADS_PER_GROUP)
    diag = w[:, :, idx, :, idx, :]
    diag = jnp.moveaxis(diag, 0, 2)
    return diag.reshape(b, N_HEADS, HEAD_DIM, HEAD_DIM)


def _layer(x, pe, wkv0, shift0, conv0, g_norm, w_in_bf16, vecs, mats, tc, proj_tm, proj_tn):
    b, t, _ = x.shape
    n_valid = tc
    if t < tc:
        n_valid = t
        x = jnp.pad(x, ((0, 0), (0, tc - t), (0, 0)))
        pe = jnp.pad(pe, ((0, 0), (0, tc - t), (0, 0)))
    tp = x.shape[1]
    proj = _in_proj(x.reshape(b * tp, D_MODEL), g_norm, w_in_bf16, proj_tm, proj_tn)
    proj = proj.reshape(b, tp, IN_COLS)
    y, s_bd, sh, cvs = _mixer(proj, x, pe, _to_block_diag(wkv0.astype(_F32)),
                              shift0.reshape(b, 1, SHIFT_COLS), conv0, vecs, mats, tc, n_valid)
    return y[:, :t], _from_block_diag(s_bd), sh.reshape(b, SHIFT_COLS), cvs


def kernel(x_prompt, x_sample, state_wkv, state_shift, state_conv, p_prompt, p_sample, g_norm, w_in, mu_shift, w_decay0, w_decay2, w_iclr0, w_iclr2, k_removal, k_replace, r_bonus, gn_w, gn_b, conv_w, w_o_a, w_o_b, w_out, g_ple, w_ple_gate, w_ple, g_final):
    assert g_norm.shape[0] == 1, "single-layer trunk"
    bp = x_prompt.shape[0]
    row = lambda a: a.reshape(1, -1).astype(_F32)
    zeros = jnp.zeros((LORA, WIDTH_A), _F32)
    w_lora = jnp.concatenate(
        [jnp.concatenate([w_decay2[0], zeros], axis=1),
         jnp.concatenate([zeros, w_iclr2[0]], axis=1)], axis=0).astype(_BF16)
    vecs = [row(mu_shift[0]), row(w_decay0[0]), row(w_iclr0[0]), row(k_removal[0]), row(k_replace[0]),
            row(r_bonus[0]), row(gn_w[0]), row(gn_b[0]), conv_w[0].astype(_F32), row(g_ple[0]), row(g_final)]
    mats = [w_lora, w_o_a[0].astype(_BF16), w_o_b[0].astype(_BF16), w_out[0].astype(_BF16),
            w_ple_gate[0].astype(_BF16), w_ple[0].astype(_BF16)]
    w_in_bf16 = w_in[0].astype(_BF16)
    g_n = row(g_norm[0])

    zero_wkv = jnp.zeros((bp, N_HEADS, HEAD_DIM, HEAD_DIM), _F32)
    zero_shift = jnp.zeros((bp, SHIFT_COLS), _F32)
    zero_conv = jnp.zeros((bp, CONV_W - 1, WIDTH_B), _F32)
    yp, wkv_p, shift_p, conv_p = _layer(
        x_prompt, p_prompt[0], zero_wkv, zero_shift, zero_conv, g_n, w_in_bf16, vecs, mats,
        tc=128, proj_tm=1024, proj_tn=1152)
    ys, wkv_s, shift_s, conv_s = _layer(
        x_sample, p_sample[0], state_wkv[0], state_shift[0], state_conv[0], g_n, w_in_bf16, vecs, mats,
        tc=CHUNK, proj_tm=512, proj_tn=1152)
    dt = x_prompt.dtype
    return (yp, ys,
            wkv_p.astype(dt)[None], shift_p[None], conv_p[None],
            wkv_s.astype(dt)[None], shift_s[None], conv_s[None])
```

```python
import functools
import math

import jax
import jax.numpy as jnp
from jax import lax
from jax.experimental import pallas as pl
from jax.experimental.pallas import tpu as pltpu

D_MODEL = 1024
HEAD_DIM = 64
N_HEADS = 16
WIDTH_A = N_HEADS * HEAD_DIM
LORA = 64
WIDTH_B = 1024
PLE_DIM = 256
CONV_W = 3
EPS = 1e-6
GN_EPS = 64e-5
DECAY_SCALE = math.exp(-0.5)
SHIFT_COLS = 3 * WIDTH_A + 2 * LORA + WIDTH_A
CONV_COLS = 4 * WIDTH_B
GATE_COLS = 2 * D_MODEL
IN_COLS = SHIFT_COLS + CONV_COLS + GATE_COLS

LANES_V7X = 128
MXU_TILE_V7X = 256
CHUNK = 64
GROUP_W = MXU_TILE_V7X
HEADS_PER_GROUP = GROUP_W // HEAD_DIM
N_GROUPS = WIDTH_A // GROUP_W
LR_COL = 3 * WIDTH_A
ZA_COL = LR_COL + 2 * LORA
PAD_TOP = 8
PROMPT_TILE = 256
PA_BLOCK = 2 * MXU_TILE_V7X
VMEM_LIMIT_BYTES = 60 * 1024 * 1024

_F32 = jnp.float32
_BF16 = jnp.bfloat16


def _sigmoid(x):
    return 0.5 * jnp.tanh(0.5 * x) + 0.5


def _dot(a, b):
    return jnp.dot(a, b, preferred_element_type=_F32)


def _dot_nt(a, b):
    return lax.dot_general(a, b, (((1,), (1,)), ((), ())), preferred_element_type=_F32)


def _dot_tn(a, b):
    return lax.dot_general(a, b, (((0,), (0,)), ((), ())), preferred_element_type=_F32)


def _rmsnorm(x, g):
    return x * lax.rsqrt(jnp.mean(x * x, axis=-1, keepdims=True) + EPS) * g


def _block_diag_stack(x, head_of_lane):
    parts = [jnp.where(head_of_lane == h, x, 0.0) for h in range(HEADS_PER_GROUP)]
    return jnp.concatenate(parts, axis=0)


def _unit_stages(load, params, wlora_ref, head_sum, s_ref, oabuf, g, r0, valid):
    c = CHUNK
    c0 = g * GROUP_W
    cs = slice(c0, c0 + GROUP_W)
    mixed, lr_act = load
    wd0_ref, wi0_ref, krem_ref, krep_ref, rb_ref, gnw_ref, gnb_ref = params

    row = lax.broadcasted_iota(jnp.int32, (c, GROUP_W), 0)
    lane = lax.broadcasted_iota(jnp.int32, (c, GROUP_W), 1)
    pos = lane & (HEAD_DIM - 1)
    head = lane >> 6
    strict = pos < row
    incl = pos <= row
    eye = pos == row
    r2 = lax.broadcasted_iota(jnp.int32, (GROUP_W, GROUP_W), 0)
    l2 = lax.broadcasted_iota(jnp.int32, (GROUP_W, GROUP_W), 1)
    same_head = (r2 >> 6) == (l2 >> 6)
    bd = functools.partial(_block_diag_stack, head_of_lane=head)

    r = mixed(c0, GROUP_W)
    k = mixed(WIDTH_A + c0, GROUP_W)
    v = mixed(2 * WIDTH_A + c0, GROUP_W)
    d_lora = _dot(lr_act, wlora_ref[:, cs])
    a_lora = _dot(lr_act, wlora_ref[:, WIDTH_A + c0:WIDTH_A + c0 + GROUP_W])
    kappa = k * krem_ref[:, cs]
    n2 = head_sum(kappa * kappa)
    yield

    lw = -DECAY_SCALE * _sigmoid(wd0_ref[:, cs] + d_lora)
    a = _sigmoid(wi0_ref[:, cs] + a_lora)
    kh = kappa * lax.rsqrt(jnp.maximum(n2, 1e-24))
    kt = k * (1.0 + (a - 1.0) * krep_ref[:, cs])
    p = a * kh
    bonus_sum = head_sum(r * kt * rb_ref[:, cs])
    if valid is not None:
        lw = jnp.where(valid, lw, 0.0)
        p = jnp.where(valid, p, 0.0)
        kt = jnp.where(valid, kt, 0.0)

    tr = lax.broadcasted_iota(jnp.int32, (c, c), 0)
    tc_ = lax.broadcasted_iota(jnp.int32, (c, c), 1)
    tri = (tc_ <= tr).astype(_BF16)
    hi = lw.astype(_BF16)
    lo = (lw - hi.astype(_F32)).astype(_BF16)
    cl2 = _dot(tri, jnp.concatenate([hi, lo], axis=1))
    yield

    cl = cl2[:, :GROUP_W] + cl2[:, GROUP_W:]
    g_end = jnp.exp(cl[c - 1:c, :])
    e_inv = jnp.exp(-cl)
    rd = r * jnp.exp(cl)
    khd = kh * jnp.exp(cl - lw)
    pd = p * e_inv
    ktd = kt * e_inv
    pdg = pd * g_end
    ktdg = ktd * g_end
    ab = _dot_nt(jnp.concatenate([khd, rd], axis=0).astype(_BF16),
                 jnp.concatenate([bd(pd), bd(ktd)], axis=0).astype(_BF16))
    yield

    b_all = jnp.where(strict, ab[:c, GROUP_W:], 0.0)
    by_all = jnp.where(incl, ab[c:, GROUP_W:], 0.0)
    ay_all = jnp.where(incl, ab[c:, :GROUP_W], 0.0)
    n = jnp.where(strict, -ab[:c, :GROUP_W], 0.0)
    t_all = jnp.where(eye, 1.0, 0.0) + n
    bv2 = _dot(jnp.concatenate([b_all, by_all], axis=0).astype(_BF16), bd(v).astype(_BF16))
    n = _dot(n.astype(_BF16), bd(n).astype(_BF16))
    yield
    for _ in range(4):
        both = _dot(jnp.concatenate([t_all, n], axis=0).astype(_BF16), bd(n).astype(_BF16))
        yield
        t_all = t_all + both[:c]
        n = both[c:]
    tn = _dot(t_all.astype(_BF16), bd(n).astype(_BF16))
    yield
    t_all = t_all + tn

    bv = bv2[:c]
    byv = bv2[c:]
    tz = _dot(t_all.astype(_BF16), jnp.concatenate([bd(khd), bd(bv)], axis=1).astype(_BF16))
    yield
    wt = tz[:, :GROUP_W]
    ut = tz[:, GROUP_W:]
    aywu = _dot(ay_all.astype(_BF16), jnp.concatenate([bd(wt), bd(ut)], axis=1).astype(_BF16))
    gmat = _dot_tn(pdg.astype(_BF16), wt.astype(_BF16))
    nt = _dot_tn(jnp.concatenate([v, ut], axis=0).astype(_BF16),
                 jnp.concatenate([ktdg, -pdg], axis=0).astype(_BF16))
    yield
    qt = (rd - aywu[:, :GROUP_W]).astype(_BF16)
    y0 = byv - aywu[:, GROUP_W:]
    gmat = jnp.where(same_head, gmat, 0.0).astype(_BF16)
    nt = jnp.where(same_head, nt, 0.0)

    s0 = s_ref[g]
    s0b = s0.astype(_BF16)
    y = _dot_nt(qt, s0b) + y0
    s_ref[g] = s0 * g_end - _dot_nt(s0b, gmat) + nt
    yield

    mean = head_sum(y) * (1.0 / HEAD_DIM)
    yield
    yc = y - mean
    var = head_sum(yc * yc) * (1.0 / HEAD_DIM)
    yield
    yn = yc * lax.rsqrt(var + GN_EPS) * gnw_ref[:, cs] + gnb_ref[:, cs]
    za = mixed(ZA_COL + c0, GROUP_W)
    oa = (yn + bonus_sum * v) * (za * _sigmoid(za))
    oabuf[r0:r0 + c, cs] = oa.astype(_BF16)


def _run_lockstep(units, start_round):
    pending = list(zip(start_round, units))
    rnd = 0
    while pending:
        alive = []
        for start, u in pending:
            if rnd >= start:
                try:
                    next(u)
                except StopIteration:
                    continue
            alive.append((start, u))
        pending = alive
        rnd += 1


def _layer_kernel(x_ref, pe_ref, s0_ref, shift0_ref, conv0_ref,
                  gn_ref, mu_ref, wd0_ref, wi0_ref, krem_ref, krep_ref, rb_ref, gnw_ref, gnb_ref,
                  cw_ref, gple_ref, gfin_ref,
                  win_ref, wlora_ref, woa_ref, wob_ref, wout_ref, wpg_ref, wple_ref,
                  y_ref, sout_ref, shout_ref, cvout_ref,
                  hbuf, pbuf, ubuf, oabuf, obbuf, mbuf, s_ref, *, tc, n_valid, stagger):
    j = pl.program_id(1)
    last = pl.num_programs(1) - 1
    c = CHUNK

    @pl.when(j == 0)
    def _():
        s_ref[...] = s0_ref[...]
        pbuf[PAD_TOP - 1:PAD_TOP, :] = shift0_ref[...]
        ubuf[PAD_TOP - 2:PAD_TOP, :] = conv0_ref[...]

    hbuf[...] = _rmsnorm(x_ref[...], gn_ref[...]).astype(_BF16)
    for col in range(0, SHIFT_COLS, PA_BLOCK):
        width = min(PA_BLOCK, SHIFT_COLS - col)
        pbuf[PAD_TOP:PAD_TOP + tc, col:col + width] = _dot(hbuf[...], win_ref[:, col:col + width])

    r2 = lax.broadcasted_iota(jnp.int32, (GROUP_W, GROUP_W), 0)
    l2 = lax.broadcasted_iota(jnp.int32, (GROUP_W, GROUP_W), 1)
    ones_bd = ((r2 >> 6) == (l2 >> 6)).astype(_BF16)

    def head_sum(x):
        return _dot(x.astype(_BF16), ones_bd)

    params = (wd0_ref, wi0_ref, krem_ref, krep_ref, rb_ref, gnw_ref, gnb_ref)

    def chunk_units(ci):
        r0 = ci * c

        def mixed(col0, width):
            cur = pbuf[PAD_TOP + r0:PAD_TOP + r0 + c, col0:col0 + width]
            prv = pbuf[PAD_TOP - 1 + r0:PAD_TOP - 1 + r0 + c, col0:col0 + width]
            return cur + mu_ref[:, col0:col0 + width] * (prv - cur)

        lr = mixed(LR_COL, 2 * LORA)
        lr_lane = lax.broadcasted_iota(jnp.int32, lr.shape, 1)
        lr_act = jnp.where(lr_lane < LORA, jnp.tanh(lr), lr).astype(_BF16)
        valid = None
        if n_valid < tc:
            valid = (lax.broadcasted_iota(jnp.int32, (c, GROUP_W), 0) + r0) < n_valid
        return [_unit_stages((mixed, lr_act), params, wlora_ref, head_sum, s_ref, oabuf, g, r0, valid)
                for g in range(N_GROUPS)]

    n_chunks = tc // c
    units = [u for ci in range(n_chunks) for u in chunk_units(ci)]
    _run_lockstep(units, [ci * stagger for ci in range(n_chunks) for _ in range(N_GROUPS)])

    @pl.when(j == last)
    def _():
        sout_ref[...] = s_ref[...]
        shout_ref[...] = pbuf[PAD_TOP + n_valid - 1:PAD_TOP + n_valid, :]

    pbuf[PAD_TOP - 1:PAD_TOP, :] = pbuf[PAD_TOP + tc - 1:PAD_TOP + tc, :]

    def proj_cols(col):
        return _dot(hbuf[...], win_ref[:, col:col + GROUP_W])

    for blk in range(0, WIDTH_B, GROUP_W):
        bs = slice(blk, blk + GROUP_W)
        gb = proj_cols(SHIFT_COLS + blk)
        gc = proj_cols(SHIFT_COLS + WIDTH_B + blk)
        xb = proj_cols(SHIFT_COLS + 2 * WIDTH_B + blk)
        zb = proj_cols(SHIFT_COLS + 3 * WIDTH_B + blk)
        u = gc * xb
        ubuf[PAD_TOP:PAD_TOP + tc, bs] = u
        cv = (cw_ref[0:1, bs] * ubuf[PAD_TOP - 2:PAD_TOP - 2 + tc, bs]
              + cw_ref[1:2, bs] * ubuf[PAD_TOP - 1:PAD_TOP - 1 + tc, bs]
              + cw_ref[2:3, bs] * u)
        obbuf[:, bs] = (gb * cv * (zb * _sigmoid(zb))).astype(_BF16)

    @pl.when(j == last)
    def _():
        cvout_ref[...] = ubuf[PAD_TOP + n_valid - 2:PAD_TOP + n_valid, :]

    ubuf[PAD_TOP - 2:PAD_TOP, :] = ubuf[PAD_TOP + tc - 2:PAD_TOP + tc, :]

    g0 = SHIFT_COLS + CONV_COLS
    for blk in range(0, D_MODEL, GROUP_W):
        bs = slice(blk, blk + GROUP_W)
        out_a = _dot(oabuf[...], woa_ref[:, bs])
        out_b = _dot(obbuf[...], wob_ref[:, bs])
        ga = _sigmoid(proj_cols(g0 + blk))
        gbm = _sigmoid(proj_cols(g0 + D_MODEL + blk))
        mbuf[:, bs] = (ga * out_a + gbm * out_b).astype(_BF16)

    x1 = x_ref[...] + _dot(mbuf[...], wout_ref[...])
    hp = _rmsnorm(x1, gple_ref[...]).astype(_BF16)
    gate = _sigmoid(_dot(hp, wpg_ref[...]))
    x2 = x1 + gate * _dot(pe_ref[...].astype(_BF16), wple_ref[...])
    y_ref[...] = _rmsnorm(x2, gfin_ref[...])


def _layer_call(x, pe, s0_bd, shift0, conv0, vecs, mats, tc, n_valid, stagger):
    b, t, _ = x.shape
    nt = t // tc
    assert n_valid == tc or nt == 1

    def tile(width):
        return pl.BlockSpec((None, tc, width), lambda bi, ji: (bi, ji, 0))

    def per_batch(shape):
        return pl.BlockSpec((None,) + shape, lambda bi, ji: (bi,) + (0,) * len(shape))

    def resident(arr):
        return pl.BlockSpec(arr.shape, lambda bi, ji: (0,) * arr.ndim, pipeline_mode=pl.Buffered(1))

    state_shape = (N_GROUPS, GROUP_W, GROUP_W)
    kern = functools.partial(_layer_kernel, tc=tc, n_valid=n_valid, stagger=stagger)
    return pl.pallas_call(
        kern,
        grid=(b, nt),
        in_specs=[tile(D_MODEL), tile(PLE_DIM),
                  per_batch(state_shape), per_batch((1, SHIFT_COLS)), per_batch((CONV_W - 1, WIDTH_B))]
                 + [resident(a) for a in vecs] + [resident(a) for a in mats],
        out_specs=[tile(D_MODEL), per_batch(state_shape), per_batch((1, SHIFT_COLS)),
                   per_batch((CONV_W - 1, WIDTH_B))],
        out_shape=[jax.ShapeDtypeStruct((b, t, D_MODEL), _F32),
                   jax.ShapeDtypeStruct((b,) + state_shape, _F32),
                   jax.ShapeDtypeStruct((b, 1, SHIFT_COLS), _F32),
                   jax.ShapeDtypeStruct((b, CONV_W - 1, WIDTH_B), _F32)],
        scratch_shapes=[pltpu.VMEM((tc, D_MODEL), _BF16),
                        pltpu.VMEM((PAD_TOP + tc, SHIFT_COLS), _F32),
                        pltpu.VMEM((PAD_TOP + tc, WIDTH_B), _F32),
                        pltpu.VMEM((tc, WIDTH_A), _BF16),
                        pltpu.VMEM((tc, WIDTH_B), _BF16),
                        pltpu.VMEM((tc, D_MODEL), _BF16),
                        pltpu.VMEM(state_shape, _F32)],
        compiler_params=pltpu.CompilerParams(
            dimension_semantics=("arbitrary", "arbitrary"),
            vmem_limit_bytes=VMEM_LIMIT_BYTES),
        name="layer",
    )(x, pe, s0_bd, shift0, conv0, *vecs, *mats)


def _to_block_diag(wkv):
    b = wkv.shape[0]
    w = wkv.reshape(b, N_GROUPS, HEADS_PER_GROUP, HEAD_DIM, HEAD_DIM)
    eye = jnp.eye(HEADS_PER_GROUP, dtype=wkv.dtype)
    out = w[:, :, :, :, None, :] * eye[None, None, :, None, :, None]
    return out.reshape(b, N_GROUPS, GROUP_W, GROUP_W)


def _from_block_diag(s_bd):
    b = s_bd.shape[0]
    w = s_bd.reshape(b, N_GROUPS, HEADS_PER_GROUP, HEAD_DIM, HEADS_PER_GROUP, HEAD_DIM)
    idx = jnp.arange(HEADS_PER_GROUP)
    diag = w[:, :, idx, :, idx, :]
    diag = jnp.moveaxis(diag, 0, 2)
    return diag.reshape(b, N_HEADS, HEAD_DIM, HEAD_DIM)


def _layer(x, pe, wkv0, shift0, conv0, vecs, mats, tc, stagger):
    b, t, _ = x.shape
    n_valid = tc
    if t < tc:
        n_valid = t
        x = jnp.pad(x, ((0, 0), (0, tc - t), (0, 0)))
        pe = jnp.pad(pe, ((0, 0), (0, tc - t), (0, 0)))
    y, s_bd, sh, cvs = _layer_call(x, pe, _to_block_diag(wkv0.astype(_F32)),
                                   shift0.reshape(b, 1, SHIFT_COLS), conv0, vecs, mats, tc, n_valid, stagger)
    return y[:, :t], _from_block_diag(s_bd), sh.reshape(b, SHIFT_COLS), cvs


def kernel(x_prompt, x_sample, state_wkv, state_shift, state_conv, p_prompt, p_sample, g_norm, w_in, mu_shift, w_decay0, w_decay2, w_iclr0, w_iclr2, k_removal, k_replace, r_bonus, gn_w, gn_b, conv_w, w_o_a, w_o_b, w_out, g_ple, w_ple_gate, w_ple, g_final):
    assert g_norm.shape[0] == 1, "single-layer trunk"
    bp = x_prompt.shape[0]
    row = lambda a: a.reshape(1, -1).astype(_F32)
    zeros = jnp.zeros((LORA, WIDTH_A), _F32)
    w_lora = jnp.concatenate(
        [jnp.concatenate([w_decay2[0], zeros], axis=1),
         jnp.concatenate([zeros, w_iclr2[0]], axis=1)], axis=0).astype(_BF16)
    vecs = [row(g_norm[0]), row(mu_shift[0]), row(w_decay0[0]), row(w_iclr0[0]), row(k_removal[0]),
            row(k_replace[0]), row(r_bonus[0]), row(gn_w[0]), row(gn_b[0]), conv_w[0].astype(_F32),
            row(g_ple[0]), row(g_final)]
    mats = [w_in[0].astype(_BF16), w_lora, w_o_a[0].astype(_BF16), w_o_b[0].astype(_BF16),
            w_out[0].astype(_BF16), w_ple_gate[0].astype(_BF16), w_ple[0].astype(_BF16)]

    zero_wkv = jnp.zeros((bp, N_HEADS, HEAD_DIM, HEAD_DIM), _F32)
    zero_shift = jnp.zeros((bp, SHIFT_COLS), _F32)
    zero_conv = jnp.zeros((bp, CONV_W - 1, WIDTH_B), _F32)
    yp, wkv_p, shift_p, conv_p = _layer(
        x_prompt, p_prompt[0], zero_wkv, zero_shift, zero_conv, vecs, mats, tc=PROMPT_TILE, stagger=0)
    ys, wkv_s, shift_s, conv_s = _layer(
        x_sample, p_sample[0], state_wkv[0], state_shift[0], state_conv[0], vecs, mats, tc=CHUNK, stagger=0)
    dt = x_prompt.dtype
    return (yp, ys,
            wkv_p.astype(dt)[None], shift_p[None], conv_p[None],
            wkv_s.astype(dt)[None], shift_s[None], conv_s[None])
```

```python
import functools
import math

import jax
import jax.numpy as jnp
from jax import lax
from jax.experimental import pallas as pl
from jax.experimental.pallas import tpu as pltpu

D_MODEL = 1024
HEAD_DIM = 64
N_HEADS = 16
WIDTH_A = N_HEADS * HEAD_DIM
LORA = 64
WIDTH_B = 1024
PLE_DIM = 256
CONV_W = 3
EPS = 1e-6
GN_EPS = 64e-5
DECAY_SCALE = math.exp(-0.5)
SHIFT_COLS = 3 * WIDTH_A + 2 * LORA + WIDTH_A
CONV_COLS = 4 * WIDTH_B
GATE_COLS = 2 * D_MODEL
IN_COLS = SHIFT_COLS + CONV_COLS + GATE_COLS

LANES_V7X = 128
MXU_TILE_V7X = 256
CHUNK = 64
GROUP_W = MXU_TILE_V7X
HEADS_PER_GROUP = GROUP_W // HEAD_DIM
N_GROUPS = WIDTH_A // GROUP_W
LR_COL = 3 * WIDTH_A
ZA_COL = LR_COL + 2 * LORA
PAD_TOP = 8
PROMPT_TILE = 256
VMEM_LIMIT_BYTES = 60 * 1024 * 1024
N_STAGE_SLOTS = 2
PREP_W = 2 * MXU_TILE_V7X

_F32 = jnp.float32
_BF16 = jnp.bfloat16


def _sigmoid(x):
    return 0.5 * jnp.tanh(0.5 * x) + 0.5


def _dot(a, b):
    return jnp.dot(a, b, preferred_element_type=_F32)


def _dot_nt(a, b):
    return lax.dot_general(a, b, (((1,), (1,)), ((), ())), preferred_element_type=_F32)


def _dot_tn(a, b):
    return lax.dot_general(a, b, (((0,), (0,)), ((), ())), preferred_element_type=_F32)


def _rmsnorm(x, g):
    return x * lax.rsqrt(jnp.mean(x * x, axis=-1, keepdims=True) + EPS) * g


def _block_diag_stack(x, head_of_lane):
    parts = [jnp.where(head_of_lane == h, x, 0.0) for h in range(HEADS_PER_GROUP)]
    return jnp.concatenate(parts, axis=0)


def _scan_stages(bufs, s_ref, g, r0):
    c = CHUNK
    rbuf, vbuf, lwbuf, khbuf, pbuf, ktbuf = bufs
    rows = slice(r0, r0 + c)
    cs = slice(g * GROUP_W, (g + 1) * GROUP_W)

    row = lax.broadcasted_iota(jnp.int32, (c, GROUP_W), 0)
    lane = lax.broadcasted_iota(jnp.int32, (c, GROUP_W), 1)
    pos = lane & (HEAD_DIM - 1)
    head = lane >> 6
    strict = pos < row
    incl = pos <= row
    eye = pos == row
    r2 = lax.broadcasted_iota(jnp.int32, (GROUP_W, GROUP_W), 0)
    l2 = lax.broadcasted_iota(jnp.int32, (GROUP_W, GROUP_W), 1)
    same_head = (r2 >> 6) == (l2 >> 6)
    bd = functools.partial(_block_diag_stack, head_of_lane=head)

    lw = lwbuf[rows, cs]
    tr = lax.broadcasted_iota(jnp.int32, (c, c), 0)
    tc_ = lax.broadcasted_iota(jnp.int32, (c, c), 1)
    tri = (tc_ <= tr).astype(_BF16)
    hi = lw.astype(_BF16)
    lo = (lw - hi.astype(_F32)).astype(_BF16)
    cl2 = _dot(tri, jnp.concatenate([hi, lo], axis=1))
    yield

    cl = cl2[:, :GROUP_W] + cl2[:, GROUP_W:]
    g_end = jnp.exp(cl[c - 1:c, :])
    e_inv = jnp.exp(-cl)
    v = vbuf[rows, cs]
    rd = rbuf[rows, cs] * jnp.exp(cl)
    khd = khbuf[rows, cs] * jnp.exp(cl - lw)
    pd = pbuf[rows, cs] * e_inv
    ktd = ktbuf[rows, cs] * e_inv
    pdg = pd * g_end
    ktdg = ktd * g_end
    ab = _dot_nt(jnp.concatenate([khd, rd], axis=0).astype(_BF16),
                 jnp.concatenate([bd(pd), bd(ktd)], axis=0).astype(_BF16))
    yield

    b_all = jnp.where(strict, ab[:c, GROUP_W:], 0.0)
    by_all = jnp.where(incl, ab[c:, GROUP_W:], 0.0)
    ay_all = jnp.where(incl, ab[c:, :GROUP_W], 0.0)
    n = jnp.where(strict, -ab[:c, :GROUP_W], 0.0)
    t_all = jnp.where(eye, 1.0, 0.0) + n
    bv2 = _dot(jnp.concatenate([b_all, by_all], axis=0).astype(_BF16), bd(v).astype(_BF16))
    n = _dot(n.astype(_BF16), bd(n).astype(_BF16))
    yield
    for _ in range(4):
        both = _dot(jnp.concatenate([t_all, n], axis=0).astype(_BF16), bd(n).astype(_BF16))
        yield
        t_all = t_all + both[:c]
        n = both[c:]
    tn = _dot(t_all.astype(_BF16), bd(n).astype(_BF16))
    yield
    t_all = t_all + tn

    bv = bv2[:c]
    byv = bv2[c:]
    tz = _dot(t_all.astype(_BF16), jnp.concatenate([bd(khd), bd(bv)], axis=1).astype(_BF16))
    yield
    wt = tz[:, :GROUP_W]
    ut = tz[:, GROUP_W:]
    aywu = _dot(ay_all.astype(_BF16), jnp.concatenate([bd(wt), bd(ut)], axis=1).astype(_BF16))
    gmat = _dot_tn(pdg.astype(_BF16), wt.astype(_BF16))
    nt = _dot_tn(jnp.concatenate([v, ut], axis=0).astype(_BF16),
                 jnp.concatenate([ktdg, -pdg], axis=0).astype(_BF16))
    yield
    qt = (rd - aywu[:, :GROUP_W]).astype(_BF16)
    y0 = byv - aywu[:, GROUP_W:]
    gmat = jnp.where(same_head, gmat, 0.0).astype(_BF16)
    nt = jnp.where(same_head, nt, 0.0)

    s0 = s_ref[g]
    s0b = s0.astype(_BF16)
    khbuf[rows, cs] = _dot_nt(qt, s0b) + y0
    s_ref[g] = s0 * g_end - _dot_nt(s0b, gmat) + nt


def _run_lockstep(units):
    units = list(units)
    while units:
        alive = []
        for u in units:
            try:
                next(u)
                alive.append(u)
            except StopIteration:
                pass
        units = alive


def _layer_kernel(x_ref, pe_ref, s0_ref, shift0_ref, conv0_ref,
                  gn_ref, mu_ref, wd0_ref, wi0_ref, krem_ref, krep_ref, rb_ref, gnw_ref, gnb_ref,
                  cw_ref, gple_ref, gfin_ref,
                  win_ref, wlora_ref, woa_ref, wob_ref, wout_ref, wpg_ref, wple_ref,
                  y_ref, sout_ref, shout_ref, cvout_ref,
                  hbuf, stage, carry, rbuf, vbuf, lwbuf, khbuf, pbuf, ktbuf, ubuf, oabuf, obbuf, mbuf, s_ref,
                  *, tc, n_valid):
    j = pl.program_id(1)
    last = pl.num_programs(1) - 1
    c = CHUNK
    head_of_lane = lax.broadcasted_iota(jnp.int32, (HEAD_DIM, GROUP_W), 1) >> 6

    @pl.when(j == 0)
    def _():
        for g in range(N_GROUPS):
            s_ref[g] = _block_diag_stack(s0_ref[g], head_of_lane)
        carry[...] = shift0_ref[...]
        ubuf[PAD_TOP - 2:PAD_TOP, :] = conv0_ref[...]

    hbuf[...] = _rmsnorm(x_ref[...], gn_ref[...]).astype(_BF16)

    stage_slot = [0]

    def shifted_mix(col, width):
        cols = slice(col, col + width)
        st = stage.at[stage_slot[0]]
        stage_slot[0] = (stage_slot[0] + 1) % N_STAGE_SLOTS
        pa = _dot(hbuf[...], win_ref[:, cols])
        st[PAD_TOP - 1:PAD_TOP, :width] = carry[:, cols]
        st[PAD_TOP:PAD_TOP + tc, :width] = pa
        prv = st[PAD_TOP - 1:PAD_TOP - 1 + tc, :width]
        carry[:, cols] = st[PAD_TOP + tc - 1:PAD_TOP + tc, :width]
        shout_ref[:, cols] = st[PAD_TOP + n_valid - 1:PAD_TOP + n_valid, :width]
        return pa + mu_ref[:, cols] * (prv - pa)

    r2 = lax.broadcasted_iota(jnp.int32, (GROUP_W, GROUP_W), 0)
    l2 = lax.broadcasted_iota(jnp.int32, (GROUP_W, GROUP_W), 1)
    ones_bd = ((r2 >> 6) == (l2 >> 6)).astype(_BF16)

    def head_sum(x):
        return _dot(x.astype(_BF16), ones_bd)

    lr = shifted_mix(LR_COL, 2 * LORA)
    lr_lane = lax.broadcasted_iota(jnp.int32, lr.shape, 1)
    lr_act = jnp.where(lr_lane < LORA, jnp.tanh(lr), lr).astype(_BF16)
    valid = None
    if n_valid < tc:
        valid = lax.broadcasted_iota(jnp.int32, (tc, PREP_W), 0) < n_valid
    for c0 in range(0, WIDTH_A, PREP_W):
        cs = slice(c0, c0 + PREP_W)
        rbuf[:, cs] = shifted_mix(c0, PREP_W)
        vbuf[:, cs] = shifted_mix(2 * WIDTH_A + c0, PREP_W)
        k = shifted_mix(WIDTH_A + c0, PREP_W)
        lw = -DECAY_SCALE * _sigmoid(wd0_ref[:, cs] + _dot(lr_act, wlora_ref[:, cs]))
        a = _sigmoid(wi0_ref[:, cs] + _dot(lr_act, wlora_ref[:, WIDTH_A + c0:WIDTH_A + c0 + PREP_W]))
        kappa = k * krem_ref[:, cs]
        ksq = kappa * kappa
        n2 = jnp.concatenate([head_sum(ksq[:, o:o + GROUP_W]) for o in range(0, PREP_W, GROUP_W)], axis=1)
        kh = kappa * lax.rsqrt(jnp.maximum(n2, 1e-24))
        kt = k * (1.0 + (a - 1.0) * krep_ref[:, cs])
        p = a * kh
        if valid is not None:
            lw = jnp.where(valid, lw, 0.0)
            p = jnp.where(valid, p, 0.0)
            kt = jnp.where(valid, kt, 0.0)
        lwbuf[:, cs] = lw
        khbuf[:, cs] = kh
        pbuf[:, cs] = p
        ktbuf[:, cs] = kt

    bufs = (rbuf, vbuf, lwbuf, khbuf, pbuf, ktbuf)
    _run_lockstep([_scan_stages(bufs, s_ref, g, ci * c)
                   for ci in range(tc // c) for g in range(N_GROUPS)])

    @pl.when(j == last)
    def _():
        for g in range(N_GROUPS):
            s = s_ref[g]
            sout_ref[g] = (s[0:HEAD_DIM] + s[HEAD_DIM:2 * HEAD_DIM]
                           + s[2 * HEAD_DIM:3 * HEAD_DIM] + s[3 * HEAD_DIM:4 * HEAD_DIM])

    for g in range(N_GROUPS):
        c0 = g * GROUP_W
        cs = slice(c0, c0 + GROUP_W)
        y = khbuf[:, cs]
        yc = y - head_sum(y) * (1.0 / HEAD_DIM)
        var = head_sum(yc * yc) * (1.0 / HEAD_DIM)
        yn = yc * lax.rsqrt(var + GN_EPS) * gnw_ref[:, cs] + gnb_ref[:, cs]
        bonus = head_sum(rbuf[:, cs] * ktbuf[:, cs] * rb_ref[:, cs]) * vbuf[:, cs]
        za = shifted_mix(ZA_COL + c0, GROUP_W)
        oabuf[:, cs] = ((yn + bonus) * (za * _sigmoid(za))).astype(_BF16)

    def proj_cols(col):
        return _dot(hbuf[...], win_ref[:, col:col + GROUP_W])

    for blk in range(0, WIDTH_B, GROUP_W):
        bs = slice(blk, blk + GROUP_W)
        gb = proj_cols(SHIFT_COLS + blk)
        gc = proj_cols(SHIFT_COLS + WIDTH_B + blk)
        xb = proj_cols(SHIFT_COLS + 2 * WIDTH_B + blk)
        zb = proj_cols(SHIFT_COLS + 3 * WIDTH_B + blk)
        u = gc * xb
        ubuf[PAD_TOP:PAD_TOP + tc, bs] = u
        cv = (cw_ref[0:1, bs] * ubuf[PAD_TOP - 2:PAD_TOP - 2 + tc, bs]
              + cw_ref[1:2, bs] * ubuf[PAD_TOP - 1:PAD_TOP - 1 + tc, bs]
              + cw_ref[2:3, bs] * u)
        obbuf[:, bs] = (gb * cv * (zb * _sigmoid(zb))).astype(_BF16)

    cvout_ref[...] = ubuf[PAD_TOP + n_valid - 2:PAD_TOP + n_valid, :]
    ubuf[PAD_TOP - 2:PAD_TOP, :] = ubuf[PAD_TOP + tc - 2:PAD_TOP + tc, :]

    g0 = SHIFT_COLS + CONV_COLS
    for blk in range(0, D_MODEL, GROUP_W):
        bs = slice(blk, blk + GROUP_W)
        out_a = _dot(oabuf[...], woa_ref[:, bs])
        out_b = _dot(obbuf[...], wob_ref[:, bs])
        ga = _sigmoid(proj_cols(g0 + blk))
        gbm = _sigmoid(proj_cols(g0 + D_MODEL + blk))
        mbuf[:, bs] = (ga * out_a + gbm * out_b).astype(_BF16)

    x1 = x_ref[...] + _dot(mbuf[...], wout_ref[...])
    hp = _rmsnorm(x1, gple_ref[...]).astype(_BF16)
    gate = _sigmoid(_dot(hp, wpg_ref[...]))
    x2 = x1 + gate * _dot(pe_ref[...].astype(_BF16), wple_ref[...])
    y_ref[...] = _rmsnorm(x2, gfin_ref[...])


def _layer_call(x, pe, s0, shift0, conv0, vecs, mats, tc, n_valid):
    b, t, _ = x.shape
    nt = t // tc
    assert n_valid == tc or nt == 1

    def tile(width):
        return pl.BlockSpec((None, tc, width), lambda bi, ji: (bi, ji, 0))

    def per_batch(shape):
        return pl.BlockSpec((None,) + shape, lambda bi, ji: (bi,) + (0,) * len(shape))

    def resident(arr):
        return pl.BlockSpec(arr.shape, lambda bi, ji: (0,) * arr.ndim, pipeline_mode=pl.Buffered(1))

    state_shape = (N_GROUPS, HEAD_DIM, GROUP_W)
    wide = lambda dtype: pltpu.VMEM((tc, WIDTH_A), dtype)
    kern = functools.partial(_layer_kernel, tc=tc, n_valid=n_valid)
    return pl.pallas_call(
        kern,
        grid=(b, nt),
        in_specs=[tile(D_MODEL), tile(PLE_DIM),
                  per_batch(state_shape), per_batch((1, SHIFT_COLS)), per_batch((CONV_W - 1, WIDTH_B))]
                 + [resident(a) for a in vecs] + [resident(a) for a in mats],
        out_specs=[tile(D_MODEL), per_batch(state_shape), per_batch((1, SHIFT_COLS)),
                   per_batch((CONV_W - 1, WIDTH_B))],
        out_shape=[jax.ShapeDtypeStruct((b, t, D_MODEL), _F32),
                   jax.ShapeDtypeStruct((b,) + state_shape, _F32),
                   jax.ShapeDtypeStruct((b, 1, SHIFT_COLS), _F32),
                   jax.ShapeDtypeStruct((b, CONV_W - 1, WIDTH_B), _F32)],
        scratch_shapes=[pltpu.VMEM((tc, D_MODEL), _BF16),
                        pltpu.VMEM((N_STAGE_SLOTS, PAD_TOP + tc, PREP_W), _F32),
                        pltpu.VMEM((1, SHIFT_COLS), _F32),
                        wide(_F32), wide(_F32), wide(_F32),
                        wide(_F32), wide(_F32), wide(_F32),
                        pltpu.VMEM((PAD_TOP + tc, WIDTH_B), _F32),
                        wide(_BF16), wide(_BF16), wide(_BF16),
                        pltpu.VMEM((N_GROUPS, GROUP_W, GROUP_W), _F32)],
        compiler_params=pltpu.CompilerParams(
            dimension_semantics=("arbitrary", "arbitrary"),
            vmem_limit_bytes=VMEM_LIMIT_BYTES),
        name="layer",
    )(x, pe, s0, shift0, conv0, *vecs, *mats)


def _group_heads(wkv):
    b = wkv.shape[0]
    w = wkv.reshape(b, N_GROUPS, HEADS_PER_GROUP, HEAD_DIM, HEAD_DIM)
    return jnp.transpose(w, (0, 1, 3, 2, 4)).reshape(b, N_GROUPS, HEAD_DIM, GROUP_W)


def _ungroup_heads(s):
    b = s.shape[0]
    w = s.reshape(b, N_GROUPS, HEAD_DIM, HEADS_PER_GROUP, HEAD_DIM)
    return jnp.transpose(w, (0, 1, 3, 2, 4)).reshape(b, N_HEADS, HEAD_DIM, HEAD_DIM)


def _layer(x, pe, wkv0, shift0, conv0, vecs, mats, tc):
    b, t, _ = x.shape
    n_valid = tc
    if t < tc:
        n_valid = t
        x = jnp.pad(x, ((0, 0), (0, tc - t), (0, 0)))
        pe = jnp.pad(pe, ((0, 0), (0, tc - t), (0, 0)))
    y, s, sh, cvs = _layer_call(x, pe, _group_heads(wkv0.astype(_F32)),
                                shift0.reshape(b, 1, SHIFT_COLS), conv0, vecs, mats, tc, n_valid)
    return y[:, :t], _ungroup_heads(s), sh.reshape(b, SHIFT_COLS), cvs


def kernel(x_prompt, x_sample, state_wkv, state_shift, state_conv, p_prompt, p_sample, g_norm, w_in, mu_shift, w_decay0, w_decay2, w_iclr0, w_iclr2, k_removal, k_replace, r_bonus, gn_w, gn_b, conv_w, w_o_a, w_o_b, w_out, g_ple, w_ple_gate, w_ple, g_final):
    assert g_norm.shape[0] == 1, "single-layer trunk"
    bp = x_prompt.shape[0]
    row = lambda a: a.reshape(1, -1).astype(_F32)
    zeros = jnp.zeros((LORA, WIDTH_A), _F32)
    w_lora = jnp.concatenate(
        [jnp.concatenate([w_decay2[0], zeros], axis=1),
         jnp.concatenate([zeros, w_iclr2[0]], axis=1)], axis=0).astype(_BF16)
    vecs = [row(g_norm[0]), row(mu_shift[0]), row(w_decay0[0]), row(w_iclr0[0]), row(k_removal[0]),
            row(k_replace[0]), row(r_bonus[0]), row(gn_w[0]), row(gn_b[0]), conv_w[0].astype(_F32),
            row(g_ple[0]), row(g_final)]
    mats = [w_in[0].astype(_BF16), w_lora, w_o_a[0].astype(_BF16), w_o_b[0].astype(_BF16),
            w_out[0].astype(_BF16), w_ple_gate[0].astype(_BF16), w_ple[0].astype(_BF16)]

    zero_wkv = jnp.zeros((bp, N_HEADS, HEAD_DIM, HEAD_DIM), _F32)
    zero_shift = jnp.zeros((bp, SHIFT_COLS), _F32)
    zero_conv = jnp.zeros((bp, CONV_W - 1, WIDTH_B), _F32)
    yp, wkv_p, shift_p, conv_p = _layer(
        x_prompt, p_prompt[0], zero_wkv, zero_shift, zero_conv, vecs, mats, tc=PROMPT_TILE)
    ys, wkv_s, shift_s, conv_s = _layer(
        x_sample, p_sample[0], state_wkv[0], state_shift[0], state_conv[0], vecs, mats, tc=CHUNK)
    dt = x_prompt.dtype
    return (yp, ys,
            wkv_p.astype(dt)[None], shift_p[None], conv_p[None],
            wkv_s.astype(dt)[None], shift_s[None], conv_s[None])
```

```python
import functools
import math

import jax
import jax.numpy as jnp
from jax import lax
from jax.experimental import pallas as pl
from jax.experimental.pallas import tpu as pltpu

D_MODEL = 1024
HEAD_DIM = 64
N_HEADS = 16
WIDTH_A = N_HEADS * HEAD_DIM
LORA = 64
WIDTH_B = 1024
PLE_DIM = 256
CONV_W = 3
EPS = 1e-6
GN_EPS = 64e-5
DECAY_SCALE = math.exp(-0.5)
SHIFT_COLS = 3 * WIDTH_A + 2 * LORA + WIDTH_A
CONV_COLS = 4 * WIDTH_B
GATE_COLS = 2 * D_MODEL
IN_COLS = SHIFT_COLS + CONV_COLS + GATE_COLS

LANES_V7X = 128
MXU_TILE_V7X = 256
CHUNK = 64
GROUP_W = MXU_TILE_V7X
HEADS_PER_GROUP = GROUP_W // HEAD_DIM
N_GROUPS = WIDTH_A // GROUP_W
LR_COL = 3 * WIDTH_A
ZA_COL = LR_COL + 2 * LORA
PAD_TOP = 8
PROMPT_TILE = 256
VMEM_LIMIT_BYTES = 60 * 1024 * 1024
N_STAGE_SLOTS = 2
PREP_W = 2 * MXU_TILE_V7X

_F32 = jnp.float32
_BF16 = jnp.bfloat16


def _sigmoid(x):
    return 0.5 * jnp.tanh(0.5 * x) + 0.5


def _dot(a, b):
    return jnp.dot(a, b, preferred_element_type=_F32)


def _dot_nt(a, b):
    return lax.dot_general(a, b, (((1,), (1,)), ((), ())), preferred_element_type=_F32)


def _dot_tn(a, b):
    return lax.dot_general(a, b, (((0,), (0,)), ((), ())), preferred_element_type=_F32)


def _rmsnorm(x, g):
    return x * lax.rsqrt(jnp.mean(x * x, axis=-1, keepdims=True) + EPS) * g


def _block_diag_stack(x, head_of_lane):
    parts = [jnp.where(head_of_lane == h, x, 0.0) for h in range(HEADS_PER_GROUP)]
    return jnp.concatenate(parts, axis=0)


def _scan_stages(bufs, s_ref, g, r0):
    c = CHUNK
    rbuf, vbuf, lwbuf, khbuf, pbuf, ktbuf = bufs
    rows = slice(r0, r0 + c)
    cs = slice(g * GROUP_W, (g + 1) * GROUP_W)

    row = lax.broadcasted_iota(jnp.int32, (c, GROUP_W), 0)
    lane = lax.broadcasted_iota(jnp.int32, (c, GROUP_W), 1)
    pos = lane & (HEAD_DIM - 1)
    head = lane >> 6
    strict = pos < row
    incl = pos <= row
    eye = pos == row
    r2 = lax.broadcasted_iota(jnp.int32, (GROUP_W, GROUP_W), 0)
    l2 = lax.broadcasted_iota(jnp.int32, (GROUP_W, GROUP_W), 1)
    same_head = (r2 >> 6) == (l2 >> 6)
    bd = functools.partial(_block_diag_stack, head_of_lane=head)

    lw = lwbuf[rows, cs]
    tr = lax.broadcasted_iota(jnp.int32, (c, c), 0)
    tc_ = lax.broadcasted_iota(jnp.int32, (c, c), 1)
    tri = (tc_ <= tr).astype(_BF16)
    hi = lw.astype(_BF16)
    lo = (lw - hi.astype(_F32)).astype(_BF16)
    cl2 = _dot(tri, jnp.concatenate([hi, lo], axis=1))
    yield

    cl = cl2[:, :GROUP_W] + cl2[:, GROUP_W:]
    g_end = jnp.exp(cl[c - 1:c, :])
    e_inv = jnp.exp(-cl)
    v = vbuf[rows, cs]
    rd = rbuf[rows, cs] * jnp.exp(cl)
    khd = khbuf[rows, cs] * jnp.exp(cl - lw)
    pd = pbuf[rows, cs] * e_inv
    ktd = ktbuf[rows, cs] * e_inv
    pdg = pd * g_end
    ktdg = ktd * g_end
    ab = _dot_nt(jnp.concatenate([khd, rd], axis=0).astype(_BF16),
                 jnp.concatenate([bd(pd), bd(ktd)], axis=0).astype(_BF16))
    yield

    b_all = jnp.where(strict, ab[:c, GROUP_W:], 0.0)
    by_all = jnp.where(incl, ab[c:, GROUP_W:], 0.0)
    ay_all = jnp.where(incl, ab[c:, :GROUP_W], 0.0)
    n = jnp.where(strict, -ab[:c, :GROUP_W], 0.0)
    t_all = jnp.where(eye, 1.0, 0.0) + n
    bv2 = _dot(jnp.concatenate([b_all, by_all], axis=0).astype(_BF16), bd(v).astype(_BF16))
    n = _dot(n.astype(_BF16), bd(n).astype(_BF16))
    yield
    for _ in range(4):
        both = _dot(jnp.concatenate([t_all, n], axis=0).astype(_BF16), bd(n).astype(_BF16))
        yield
        t_all = t_all + both[:c]
        n = both[c:]
    tn = _dot(t_all.astype(_BF16), bd(n).astype(_BF16))
    yield
    t_all = t_all + tn

    bv = bv2[:c]
    byv = bv2[c:]
    ayt = _dot(ay_all.astype(_BF16), bd(t_all).astype(_BF16))
    yield
    tz = _dot(jnp.concatenate([t_all, ayt], axis=0).astype(_BF16),
              jnp.concatenate([bd(khd), bd(bv)], axis=1).astype(_BF16))
    yield
    wt = tz[:c, :GROUP_W]
    ut = tz[:c, GROUP_W:]
    qt = rd - tz[c:, :GROUP_W]
    y0 = byv - tz[c:, GROUP_W:]

    s0 = s_ref[g]
    yz = _dot_nt(jnp.concatenate([qt, wt], axis=0).astype(_BF16), s0.astype(_BF16))
    khbuf[rows, cs] = yz[:c] + y0
    zt = yz[c:] + ut
    upd = _dot_tn(jnp.concatenate([v, -zt], axis=0).astype(_BF16),
                  jnp.concatenate([ktdg, pdg], axis=0).astype(_BF16))
    s_ref[g] = s0 * g_end + jnp.where(same_head, upd, 0.0)


def _run_lockstep(units):
    units = list(units)
    while units:
        alive = []
        for u in units:
            try:
                next(u)
                alive.append(u)
            except StopIteration:
                pass
        units = alive


def _layer_kernel(x_ref, pe_ref, s0_ref, shift0_ref, conv0_ref,
                  gn_ref, mu_ref, wd0_ref, wi0_ref, krem_ref, krep_ref, rb_ref, gnw_ref, gnb_ref,
                  cw_ref, gple_ref, gfin_ref,
                  win_ref, wlora_ref, woa_ref, wob_ref, wout_ref, wpg_ref, wple_ref,
                  y_ref, sout_ref, shout_ref, cvout_ref,
                  hbuf, stage, carry, rbuf, vbuf, lwbuf, khbuf, pbuf, ktbuf, ubuf, oabuf, obbuf, mbuf, s_ref,
                  *, tc, n_valid):
    j = pl.program_id(1)
    last = pl.num_programs(1) - 1
    c = CHUNK
    head_of_lane = lax.broadcasted_iota(jnp.int32, (HEAD_DIM, GROUP_W), 1) >> 6

    @pl.when(j == 0)
    def _():
        for g in range(N_GROUPS):
            s_ref[g] = _block_diag_stack(s0_ref[g], head_of_lane)
        carry[...] = shift0_ref[...]
        ubuf[PAD_TOP - 2:PAD_TOP, :] = conv0_ref[...]

    hbuf[...] = _rmsnorm(x_ref[...], gn_ref[...]).astype(_BF16)

    stage_slot = [0]

    def shifted_mix(col, width):
        cols = slice(col, col + width)
        st = stage.at[stage_slot[0]]
        stage_slot[0] = (stage_slot[0] + 1) % N_STAGE_SLOTS
        pa = _dot(hbuf[...], win_ref[:, cols])
        st[PAD_TOP - 1:PAD_TOP, :width] = carry[:, cols]
        st[PAD_TOP:PAD_TOP + tc, :width] = pa
        prv = st[PAD_TOP - 1:PAD_TOP - 1 + tc, :width]
        carry[:, cols] = st[PAD_TOP + tc - 1:PAD_TOP + tc, :width]
        shout_ref[:, cols] = st[PAD_TOP + n_valid - 1:PAD_TOP + n_valid, :width]
        return pa + mu_ref[:, cols] * (prv - pa)

    r2 = lax.broadcasted_iota(jnp.int32, (GROUP_W, GROUP_W), 0)
    l2 = lax.broadcasted_iota(jnp.int32, (GROUP_W, GROUP_W), 1)
    ones_bd = ((r2 >> 6) == (l2 >> 6)).astype(_BF16)

    def head_sum(x):
        return _dot(x.astype(_BF16), ones_bd)

    lr = shifted_mix(LR_COL, 2 * LORA)
    lr_lane = lax.broadcasted_iota(jnp.int32, lr.shape, 1)
    lr_act = jnp.where(lr_lane < LORA, jnp.tanh(lr), lr).astype(_BF16)
    valid = None
    if n_valid < tc:
        valid = lax.broadcasted_iota(jnp.int32, (tc, PREP_W), 0) < n_valid
    for c0 in range(0, WIDTH_A, PREP_W):
        cs = slice(c0, c0 + PREP_W)
        rbuf[:, cs] = shifted_mix(c0, PREP_W)
        vbuf[:, cs] = shifted_mix(2 * WIDTH_A + c0, PREP_W)
        k = shifted_mix(WIDTH_A + c0, PREP_W)
        lw = -DECAY_SCALE * _sigmoid(wd0_ref[:, cs] + _dot(lr_act, wlora_ref[:, cs]))
        a = _sigmoid(wi0_ref[:, cs] + _dot(lr_act, wlora_ref[:, WIDTH_A + c0:WIDTH_A + c0 + PREP_W]))
        kappa = k * krem_ref[:, cs]
        ksq = kappa * kappa
        n2 = jnp.concatenate([head_sum(ksq[:, o:o + GROUP_W]) for o in range(0, PREP_W, GROUP_W)], axis=1)
        kh = kappa * lax.rsqrt(jnp.maximum(n2, 1e-24))
        kt = k * (1.0 + (a - 1.0) * krep_ref[:, cs])
        p = a * kh
        if valid is not None:
            lw = jnp.where(valid, lw, 0.0)
            p = jnp.where(valid, p, 0.0)
            kt = jnp.where(valid, kt, 0.0)
        lwbuf[:, cs] = lw
        khbuf[:, cs] = kh
        pbuf[:, cs] = p
        ktbuf[:, cs] = kt

    bufs = (rbuf, vbuf, lwbuf, khbuf, pbuf, ktbuf)
    _run_lockstep([_scan_stages(bufs, s_ref, g, ci * c)
                   for ci in range(tc // c) for g in range(N_GROUPS)])

    @pl.when(j == last)
    def _():
        for g in range(N_GROUPS):
            s = s_ref[g]
            sout_ref[g] = (s[0:HEAD_DIM] + s[HEAD_DIM:2 * HEAD_DIM]
                           + s[2 * HEAD_DIM:3 * HEAD_DIM] + s[3 * HEAD_DIM:4 * HEAD_DIM])

    for g in range(N_GROUPS):
        c0 = g * GROUP_W
        cs = slice(c0, c0 + GROUP_W)
        y = khbuf[:, cs]
        yc = y - head_sum(y) * (1.0 / HEAD_DIM)
        var = head_sum(yc * yc) * (1.0 / HEAD_DIM)
        yn = yc * lax.rsqrt(var + GN_EPS) * gnw_ref[:, cs] + gnb_ref[:, cs]
        bonus = head_sum(rbuf[:, cs] * ktbuf[:, cs] * rb_ref[:, cs]) * vbuf[:, cs]
        za = shifted_mix(ZA_COL + c0, GROUP_W)
        oabuf[:, cs] = ((yn + bonus) * (za * _sigmoid(za))).astype(_BF16)

    def proj_cols(col):
        return _dot(hbuf[...], win_ref[:, col:col + GROUP_W])

    for blk in range(0, WIDTH_B, GROUP_W):
        bs = slice(blk, blk + GROUP_W)
        gb = proj_cols(SHIFT_COLS + blk)
        gc = proj_cols(SHIFT_COLS + WIDTH_B + blk)
        xb = proj_cols(SHIFT_COLS + 2 * WIDTH_B + blk)
        zb = proj_cols(SHIFT_COLS + 3 * WIDTH_B + blk)
        u = gc * xb
        ubuf[PAD_TOP:PAD_TOP + tc, bs] = u
        cv = (cw_ref[0:1, bs] * ubuf[PAD_TOP - 2:PAD_TOP - 2 + tc, bs]
              + cw_ref[1:2, bs] * ubuf[PAD_TOP - 1:PAD_TOP - 1 + tc, bs]
              + cw_ref[2:3, bs] * u)
        obbuf[:, bs] = (gb * cv * (zb * _sigmoid(zb))).astype(_BF16)

    cvout_ref[...] = ubuf[PAD_TOP + n_valid - 2:PAD_TOP + n_valid, :]
    ubuf[PAD_TOP - 2:PAD_TOP, :] = ubuf[PAD_TOP + tc - 2:PAD_TOP + tc, :]

    g0 = SHIFT_COLS + CONV_COLS
    for blk in range(0, D_MODEL, GROUP_W):
        bs = slice(blk, blk + GROUP_W)
        out_a = _dot(oabuf[...], woa_ref[:, bs])
        out_b = _dot(obbuf[...], wob_ref[:, bs])
        ga = _sigmoid(proj_cols(g0 + blk))
        gbm = _sigmoid(proj_cols(g0 + D_MODEL + blk))
        mbuf[:, bs] = (ga * out_a + gbm * out_b).astype(_BF16)

    x1 = x_ref[...] + _dot(mbuf[...], wout_ref[...])
    hp = _rmsnorm(x1, gple_ref[...]).astype(_BF16)
    gate = _sigmoid(_dot(hp, wpg_ref[...]))
    x2 = x1 + gate * _dot(pe_ref[...].astype(_BF16), wple_ref[...])
    y_ref[...] = _rmsnorm(x2, gfin_ref[...])


def _layer_call(x, pe, s0, shift0, conv0, vecs, mats, tc, n_valid):
    b, t, _ = x.shape
    nt = t // tc
    assert n_valid == tc or nt == 1

    def tile(width):
        return pl.BlockSpec((None, tc, width), lambda bi, ji: (bi, ji, 0))

    def per_batch(shape):
        return pl.BlockSpec((None,) + shape, lambda bi, ji: (bi,) + (0,) * len(shape))

    def resident(arr):
        return pl.BlockSpec(arr.shape, lambda bi, ji: (0,) * arr.ndim, pipeline_mode=pl.Buffered(1))

    state_shape = (N_GROUPS, HEAD_DIM, GROUP_W)
    wide = lambda dtype: pltpu.VMEM((tc, WIDTH_A), dtype)
    kern = functools.partial(_layer_kernel, tc=tc, n_valid=n_valid)
    return pl.pallas_call(
        kern,
        grid=(b, nt),
        in_specs=[tile(D_MODEL), tile(PLE_DIM),
                  per_batch(state_shape), per_batch((1, SHIFT_COLS)), per_batch((CONV_W - 1, WIDTH_B))]
                 + [resident(a) for a in vecs] + [resident(a) for a in mats],
        out_specs=[tile(D_MODEL), per_batch(state_shape), per_batch((1, SHIFT_COLS)),
                   per_batch((CONV_W - 1, WIDTH_B))],
        out_shape=[jax.ShapeDtypeStruct((b, t, D_MODEL), _F32),
                   jax.ShapeDtypeStruct((b,) + state_shape, _F32),
                   jax.ShapeDtypeStruct((b, 1, SHIFT_COLS), _F32),
                   jax.ShapeDtypeStruct((b, CONV_W - 1, WIDTH_B), _F32)],
        scratch_shapes=[pltpu.VMEM((tc, D_MODEL), _BF16),
                        pltpu.VMEM((N_STAGE_SLOTS, PAD_TOP + tc, PREP_W), _F32),
                        pltpu.VMEM((1, SHIFT_COLS), _F32),
                        wide(_F32), wide(_F32), wide(_F32),
                        wide(_F32), wide(_F32), wide(_F32),
                        pltpu.VMEM((PAD_TOP + tc, WIDTH_B), _F32),
                        wide(_BF16), wide(_BF16), wide(_BF16),
                        pltpu.VMEM((N_GROUPS, GROUP_W, GROUP_W), _F32)],
        compiler_params=pltpu.CompilerParams(
            dimension_semantics=("arbitrary", "arbitrary"),
            vmem_limit_bytes=VMEM_LIMIT_BYTES),
        name="layer",
    )(x, pe, s0, shift0, conv0, *vecs, *mats)


def _group_heads(wkv):
    b = wkv.shape[0]
    w = wkv.reshape(b, N_GROUPS, HEADS_PER_GROUP, HEAD_DIM, HEAD_DIM)
    return jnp.transpose(w, (0, 1, 3, 2, 4)).reshape(b, N_GROUPS, HEAD_DIM, GROUP_W)


def _ungroup_heads(s):
    b = s.shape[0]
    w = s.reshape(b, N_GROUPS, HEAD_DIM, HEADS_PER_GROUP, HEAD_DIM)
    return jnp.transpose(w, (0, 1, 3, 2, 4)).reshape(b, N_HEADS, HEAD_DIM, HEAD_DIM)


def _layer(x, pe, wkv0, shift0, conv0, vecs, mats, tc):
    b, t, _ = x.shape
    n_valid = tc
    if t < tc:
        n_valid = t
        x = jnp.pad(x, ((0, 0), (0, tc - t), (0, 0)))
        pe = jnp.pad(pe, ((0, 0), (0, tc - t), (0, 0)))
    y, s, sh, cvs = _layer_call(x, pe, _group_heads(wkv0.astype(_F32)),
                                shift0.reshape(b, 1, SHIFT_COLS), conv0, vecs, mats, tc, n_valid)
    return y[:, :t], _ungroup_heads(s), sh.reshape(b, SHIFT_COLS), cvs


def kernel(x_prompt, x_sample, state_wkv, state_shift, state_conv, p_prompt, p_sample, g_norm, w_in, mu_shift, w_decay0, w_decay2, w_iclr0, w_iclr2, k_removal, k_replace, r_bonus, gn_w, gn_b, conv_w, w_o_a, w_o_b, w_out, g_ple, w_ple_gate, w_ple, g_final):
    assert g_norm.shape[0] == 1, "single-layer trunk"
    bp = x_prompt.shape[0]
    row = lambda a: a.reshape(1, -1).astype(_F32)
    zeros = jnp.zeros((LORA, WIDTH_A), _F32)
    w_lora = jnp.concatenate(
        [jnp.concatenate([w_decay2[0], zeros], axis=1),
         jnp.concatenate([zeros, w_iclr2[0]], axis=1)], axis=0).astype(_BF16)
    vecs = [row(g_norm[0]), row(mu_shift[0]), row(w_decay0[0]), row(w_iclr0[0]), row(k_removal[0]),
            row(k_replace[0]), row(r_bonus[0]), row(gn_w[0]), row(gn_b[0]), conv_w[0].astype(_F32),
            row(g_ple[0]), row(g_final)]
    mats = [w_in[0].astype(_BF16), w_lora, w_o_a[0].astype(_BF16), w_o_b[0].astype(_BF16),
            w_out[0].astype(_BF16), w_ple_gate[0].astype(_BF16), w_ple[0].astype(_BF16)]

    zero_wkv = jnp.zeros((bp, N_HEADS, HEAD_DIM, HEAD_DIM), _F32)
    zero_shift = jnp.zeros((bp, SHIFT_COLS), _F32)
    zero_conv = jnp.zeros((bp, CONV_W - 1, WIDTH_B), _F32)
    yp, wkv_p, shift_p, conv_p = _layer(
        x_prompt, p_prompt[0], zero_wkv, zero_shift, zero_conv, vecs, mats, tc=PROMPT_TILE)
    ys, wkv_s, shift_s, conv_s = _layer(
        x_sample, p_sample[0], state_wkv[0], state_shift[0], state_conv[0], vecs, mats, tc=CHUNK)
    dt = x_prompt.dtype
    return (yp, ys,
            wkv_p.astype(dt)[None], shift_p[None], conv_p[None],
            wkv_s.astype(dt)[None], shift_s[None], conv_s[None])
```

```python
import functools
import math

import jax
import jax.numpy as jnp
from jax import lax
from jax.experimental import pallas as pl
from jax.experimental.pallas import tpu as pltpu

D_MODEL = 1024
HEAD_DIM = 64
N_HEADS = 16
WIDTH_A = N_HEADS * HEAD_DIM
LORA = 64
WIDTH_B = 1024
PLE_DIM = 256
CONV_W = 3
EPS = 1e-6
GN_EPS = 64e-5
DECAY_SCALE = math.exp(-0.5)
SHIFT_COLS = 3 * WIDTH_A + 2 * LORA + WIDTH_A
CONV_COLS = 4 * WIDTH_B
GATE_COLS = 2 * D_MODEL
IN_COLS = SHIFT_COLS + CONV_COLS + GATE_COLS

LANES_V7X = 128
MXU_TILE_V7X = 256
CHUNK = 64
GROUP_W = MXU_TILE_V7X
HEADS_PER_GROUP = GROUP_W // HEAD_DIM
N_GROUPS = WIDTH_A // GROUP_W
LR_COL = 3 * WIDTH_A
ZA_COL = LR_COL + 2 * LORA
PAD_TOP = 8
PROMPT_TILE = 256
SAMPLE_SEQS_PER_STEP = 4
VMEM_LIMIT_BYTES = 60 * 1024 * 1024
N_STAGE_SLOTS = 2
PREP_W = 2 * MXU_TILE_V7X

_F32 = jnp.float32
_BF16 = jnp.bfloat16


def _sigmoid(x):
    return 0.5 * jnp.tanh(0.5 * x) + 0.5


def _dot(a, b):
    return jnp.dot(a, b, preferred_element_type=_F32)


def _dot_nt(a, b):
    return lax.dot_general(a, b, (((1,), (1,)), ((), ())), preferred_element_type=_F32)


def _dot_tn(a, b):
    return lax.dot_general(a, b, (((0,), (0,)), ((), ())), preferred_element_type=_F32)


def _rmsnorm(x, g):
    return x * lax.rsqrt(jnp.mean(x * x, axis=-1, keepdims=True) + EPS) * g


def _block_diag_stack(x, head_of_lane):
    parts = [jnp.where(head_of_lane == h, x, 0.0) for h in range(HEADS_PER_GROUP)]
    return jnp.concatenate(parts, axis=0)


def _scan_stages(bufs, s_ref, si, g, r0):
    c = CHUNK
    rbuf, vbuf, lwbuf, khbuf, pbuf, ktbuf = bufs
    rows = slice(r0, r0 + c)
    cs = slice(g * GROUP_W, (g + 1) * GROUP_W)

    row = lax.broadcasted_iota(jnp.int32, (c, GROUP_W), 0)
    lane = lax.broadcasted_iota(jnp.int32, (c, GROUP_W), 1)
    pos = lane & (HEAD_DIM - 1)
    head = lane >> 6
    strict = pos < row
    incl = pos <= row
    eye = pos == row
    r2 = lax.broadcasted_iota(jnp.int32, (GROUP_W, GROUP_W), 0)
    l2 = lax.broadcasted_iota(jnp.int32, (GROUP_W, GROUP_W), 1)
    same_head = (r2 >> 6) == (l2 >> 6)
    bd = functools.partial(_block_diag_stack, head_of_lane=head)

    lw = lwbuf[rows, cs]
    tr = lax.broadcasted_iota(jnp.int32, (c, c), 0)
    tc_ = lax.broadcasted_iota(jnp.int32, (c, c), 1)
    tri = (tc_ <= tr).astype(_BF16)
    hi = lw.astype(_BF16)
    lo = (lw - hi.astype(_F32)).astype(_BF16)
    cl2 = _dot(tri, jnp.concatenate([hi, lo], axis=1))
    yield

    cl = cl2[:, :GROUP_W] + cl2[:, GROUP_W:]
    g_end = jnp.exp(cl[c - 1:c, :])
    e_inv = jnp.exp(-cl)
    v = vbuf[rows, cs]
    rd = rbuf[rows, cs] * jnp.exp(cl)
    khd = khbuf[rows, cs] * jnp.exp(cl - lw)
    pd = pbuf[rows, cs] * e_inv
    ktd = ktbuf[rows, cs] * e_inv
    pdg = pd * g_end
    ktdg = ktd * g_end
    ab = _dot_nt(jnp.concatenate([khd, rd], axis=0).astype(_BF16),
                 jnp.concatenate([bd(pd), bd(ktd)], axis=0).astype(_BF16))
    yield

    b_all = jnp.where(strict, ab[:c, GROUP_W:], 0.0)
    by_all = jnp.where(incl, ab[c:, GROUP_W:], 0.0)
    ay_all = jnp.where(incl, ab[c:, :GROUP_W], 0.0)
    n = jnp.where(strict, -ab[:c, :GROUP_W], 0.0)
    t_all = jnp.where(eye, 1.0, 0.0) + n
    bv2 = _dot(jnp.concatenate([b_all, by_all], axis=0).astype(_BF16), bd(v).astype(_BF16))
    n = _dot(n.astype(_BF16), bd(n).astype(_BF16))
    yield
    for _ in range(4):
        both = _dot(jnp.concatenate([t_all, n], axis=0).astype(_BF16), bd(n).astype(_BF16))
        yield
        t_all = t_all + both[:c]
        n = both[c:]
    tn = _dot(t_all.astype(_BF16), bd(n).astype(_BF16))
    yield
    t_all = t_all + tn

    bv = bv2[:c]
    byv = bv2[c:]
    tz = _dot(t_all.astype(_BF16), jnp.concatenate([bd(khd), bd(bv)], axis=1).astype(_BF16))
    yield
    wt = tz[:, :GROUP_W]
    ut = tz[:, GROUP_W:]
    aywu = _dot(ay_all.astype(_BF16), jnp.concatenate([bd(wt), bd(ut)], axis=1).astype(_BF16))
    gmat = _dot_tn(pdg.astype(_BF16), wt.astype(_BF16))
    nt = _dot_tn(jnp.concatenate([v, ut], axis=0).astype(_BF16),
                 jnp.concatenate([ktdg, -pdg], axis=0).astype(_BF16))
    yield
    qt = (rd - aywu[:, :GROUP_W]).astype(_BF16)
    y0 = byv - aywu[:, GROUP_W:]
    gmat = jnp.where(same_head, gmat, 0.0).astype(_BF16)
    nt = jnp.where(same_head, nt, 0.0)

    s0 = s_ref[si]
    s0b = s0.astype(_BF16)
    khbuf[rows, cs] = _dot_nt(qt, s0b) + y0
    s_ref[si] = s0 * g_end - _dot_nt(s0b, gmat) + nt


def _run_lockstep(units):
    units = list(units)
    while units:
        alive = []
        for u in units:
            try:
                next(u)
                alive.append(u)
            except StopIteration:
                pass
        units = alive


def _layer_kernel(x_ref, pe_ref, s0_ref, shift0_ref, conv0_ref,
                  gn_ref, mu_ref, wd0_ref, wi0_ref, krem_ref, krep_ref, rb_ref, gnw_ref, gnb_ref,
                  cw_ref, gple_ref, gfin_ref,
                  win_ref, wlora_ref, woa_ref, wob_ref, wout_ref, wpg_ref, wple_ref,
                  y_ref, sout_ref, shout_ref, cvout_ref,
                  hbuf, stage, carry, tails, rbuf, vbuf, lwbuf, khbuf, pbuf, ktbuf, ubuf, oabuf, obbuf, mbuf, s_ref,
                  *, nseq, ts, n_valid):
    tc = nseq * ts
    j = pl.program_id(1)
    last = pl.num_programs(1) - 1
    c = CHUNK
    head_of_lane = lax.broadcasted_iota(jnp.int32, (HEAD_DIM, GROUP_W), 1) >> 6

    @pl.when(j == 0)
    def _():
        for q in range(nseq):
            for g in range(N_GROUPS):
                s_ref[q * N_GROUPS + g] = _block_diag_stack(s0_ref[q, g], head_of_lane)
            carry[q:q + 1, :] = shift0_ref[q]
            tails[q] = conv0_ref[q]

    hbuf[...] = _rmsnorm(x_ref[...], gn_ref[...]).astype(_BF16)

    stage_slot = [0]

    def shifted_mix(col, width):
        cols = slice(col, col + width)
        st = stage.at[stage_slot[0]]
        stage_slot[0] = (stage_slot[0] + 1) % N_STAGE_SLOTS
        pa = _dot(hbuf[...], win_ref[:, cols])
        st[PAD_TOP:PAD_TOP + tc, :width] = pa
        new_last = [st[PAD_TOP + q * ts + ts - 1:PAD_TOP + q * ts + ts, :width] for q in range(nseq)]
        for q in range(nseq):
            shout_ref[q, :, cols] = st[PAD_TOP + q * ts + n_valid - 1:PAD_TOP + q * ts + n_valid, :width]
        for q in range(nseq):
            st[PAD_TOP + q * ts - 1:PAD_TOP + q * ts, :width] = carry[q:q + 1, cols]
        prv = st[PAD_TOP - 1:PAD_TOP - 1 + tc, :width]
        for q in range(nseq):
            carry[q:q + 1, cols] = new_last[q]
        return pa + mu_ref[:, cols] * (prv - pa)

    r2 = lax.broadcasted_iota(jnp.int32, (GROUP_W, GROUP_W), 0)
    l2 = lax.broadcasted_iota(jnp.int32, (GROUP_W, GROUP_W), 1)
    ones_bd = ((r2 >> 6) == (l2 >> 6)).astype(_BF16)

    def head_sum(x):
        return _dot(x.astype(_BF16), ones_bd)

    lr = shifted_mix(LR_COL, 2 * LORA)
    lr_lane = lax.broadcasted_iota(jnp.int32, lr.shape, 1)
    lr_act = jnp.where(lr_lane < LORA, jnp.tanh(lr), lr).astype(_BF16)
    valid = None
    if n_valid < ts:
        valid = lax.rem(lax.broadcasted_iota(jnp.int32, (tc, PREP_W), 0), ts) < n_valid
    for c0 in range(0, WIDTH_A, PREP_W):
        cs = slice(c0, c0 + PREP_W)
        rbuf[:, cs] = shifted_mix(c0, PREP_W)
        vbuf[:, cs] = shifted_mix(2 * WIDTH_A + c0, PREP_W)
        k = shifted_mix(WIDTH_A + c0, PREP_W)
        lw = -DECAY_SCALE * _sigmoid(wd0_ref[:, cs] + _dot(lr_act, wlora_ref[:, cs]))
        a = _sigmoid(wi0_ref[:, cs] + _dot(lr_act, wlora_ref[:, WIDTH_A + c0:WIDTH_A + c0 + PREP_W]))
        kappa = k * krem_ref[:, cs]
        ksq = kappa * kappa
        n2 = jnp.concatenate([head_sum(ksq[:, o:o + GROUP_W]) for o in range(0, PREP_W, GROUP_W)], axis=1)
        kh = kappa * lax.rsqrt(jnp.maximum(n2, 1e-24))
        kt = k * (1.0 + (a - 1.0) * krep_ref[:, cs])
        p = a * kh
        if valid is not None:
            lw = jnp.where(valid, lw, 0.0)
            p = jnp.where(valid, p, 0.0)
            kt = jnp.where(valid, kt, 0.0)
        lwbuf[:, cs] = lw
        khbuf[:, cs] = kh
        pbuf[:, cs] = p
        ktbuf[:, cs] = kt

    bufs = (rbuf, vbuf, lwbuf, khbuf, pbuf, ktbuf)
    _run_lockstep([_scan_stages(bufs, s_ref, q * N_GROUPS + g, g, q * ts + ci * c)
                   for q in range(nseq) for ci in range(ts // c) for g in range(N_GROUPS)])

    @pl.when(j == last)
    def _():
        for q in range(nseq):
            for g in range(N_GROUPS):
                s = s_ref[q * N_GROUPS + g]
                sout_ref[q, g] = (s[0:HEAD_DIM] + s[HEAD_DIM:2 * HEAD_DIM]
                                  + s[2 * HEAD_DIM:3 * HEAD_DIM] + s[3 * HEAD_DIM:4 * HEAD_DIM])

    for g in range(N_GROUPS):
        c0 = g * GROUP_W
        cs = slice(c0, c0 + GROUP_W)
        y = khbuf[:, cs]
        yc = y - head_sum(y) * (1.0 / HEAD_DIM)
        var = head_sum(yc * yc) * (1.0 / HEAD_DIM)
        yn = yc * lax.rsqrt(var + GN_EPS) * gnw_ref[:, cs] + gnb_ref[:, cs]
        bonus = head_sum(rbuf[:, cs] * ktbuf[:, cs] * rb_ref[:, cs]) * vbuf[:, cs]
        za = shifted_mix(ZA_COL + c0, GROUP_W)
        oabuf[:, cs] = ((yn + bonus) * (za * _sigmoid(za))).astype(_BF16)

    def proj_cols(col):
        return _dot(hbuf[...], win_ref[:, col:col + GROUP_W])

    for blk in range(0, WIDTH_B, GROUP_W):
        bs = slice(blk, blk + GROUP_W)
        gb = proj_cols(SHIFT_COLS + blk)
        gc = proj_cols(SHIFT_COLS + WIDTH_B + blk)
        xb = proj_cols(SHIFT_COLS + 2 * WIDTH_B + blk)
        zb = proj_cols(SHIFT_COLS + 3 * WIDTH_B + blk)
        u = gc * xb
        ubuf[PAD_TOP:PAD_TOP + tc, bs] = u
        new_tail = [ubuf[PAD_TOP + q * ts + ts - 2:PAD_TOP + q * ts + ts, bs] for q in range(nseq)]
        for q in range(nseq):
            cvout_ref[q, :, bs] = ubuf[PAD_TOP + q * ts + n_valid - 2:PAD_TOP + q * ts + n_valid, bs]
        for q in range(nseq):
            ubuf[PAD_TOP + q * ts - 2:PAD_TOP + q * ts, bs] = tails[q, :, bs]
        cv = (cw_ref[0:1, bs] * ubuf[PAD_TOP - 2:PAD_TOP - 2 + tc, bs]
              + cw_ref[1:2, bs] * ubuf[PAD_TOP - 1:PAD_TOP - 1 + tc, bs]
              + cw_ref[2:3, bs] * u)
        for q in range(nseq):
            tails[q, :, bs] = new_tail[q]
        obbuf[:, bs] = (gb * cv * (zb * _sigmoid(zb))).astype(_BF16)

    g0 = SHIFT_COLS + CONV_COLS
    for blk in range(0, D_MODEL, GROUP_W):
        bs = slice(blk, blk + GROUP_W)
        out_a = _dot(oabuf[...], woa_ref[:, bs])
        out_b = _dot(obbuf[...], wob_ref[:, bs])
        ga = _sigmoid(proj_cols(g0 + blk))
        gbm = _sigmoid(proj_cols(g0 + D_MODEL + blk))
        mbuf[:, bs] = (ga * out_a + gbm * out_b).astype(_BF16)

    x1 = x_ref[...] + _dot(mbuf[...], wout_ref[...])
    hp = _rmsnorm(x1, gple_ref[...]).astype(_BF16)
    gate = _sigmoid(_dot(hp, wpg_ref[...]))
    x2 = x1 + gate * _dot(pe_ref[...].astype(_BF16), wple_ref[...])
    y_ref[...] = _rmsnorm(x2, gfin_ref[...])


def _layer_call(x2d, pe2d, s0, shift0, conv0, vecs, mats, nseq, ts, nt, n_valid):
    b = s0.shape[0]
    tc = nseq * ts
    assert nseq == 1 or (nt == 1 and n_valid <= ts - (CONV_W - 1))
    assert n_valid == ts or nt == 1

    def tile(width):
        return pl.BlockSpec((tc, width), lambda bi, ji: (bi * nt + ji, 0))

    def per_seq(shape):
        return pl.BlockSpec((nseq,) + shape, lambda bi, ji: (bi,) + (0,) * len(shape))

    def resident(arr):
        return pl.BlockSpec(arr.shape, lambda bi, ji: (0,) * arr.ndim, pipeline_mode=pl.Buffered(1))

    state_shape = (N_GROUPS, HEAD_DIM, GROUP_W)
    wide = lambda dtype: pltpu.VMEM((tc, WIDTH_A), dtype)
    kern = functools.partial(_layer_kernel, nseq=nseq, ts=ts, n_valid=n_valid)
    return pl.pallas_call(
        kern,
        grid=(b // nseq, nt),
        in_specs=[tile(D_MODEL), tile(PLE_DIM),
                  per_seq(state_shape), per_seq((1, SHIFT_COLS)), per_seq((CONV_W - 1, WIDTH_B))]
                 + [resident(a) for a in vecs] + [resident(a) for a in mats],
        out_specs=[tile(D_MODEL), per_seq(state_shape), per_seq((1, SHIFT_COLS)),
                   per_seq((CONV_W - 1, WIDTH_B))],
        out_shape=[jax.ShapeDtypeStruct(x2d.shape, _F32),
                   jax.ShapeDtypeStruct((b,) + state_shape, _F32),
                   jax.ShapeDtypeStruct((b, 1, SHIFT_COLS), _F32),
                   jax.ShapeDtypeStruct((b, CONV_W - 1, WIDTH_B), _F32)],
        scratch_shapes=[pltpu.VMEM((tc, D_MODEL), _BF16),
                        pltpu.VMEM((N_STAGE_SLOTS, PAD_TOP + tc, PREP_W), _F32),
                        pltpu.VMEM((nseq, SHIFT_COLS), _F32),
                        pltpu.VMEM((nseq, CONV_W - 1, WIDTH_B), _F32),
                        wide(_F32), wide(_F32), wide(_F32),
                        wide(_F32), wide(_F32), wide(_F32),
                        pltpu.VMEM((PAD_TOP + tc, WIDTH_B), _F32),
                        wide(_BF16), wide(_BF16), wide(_BF16),
                        pltpu.VMEM((nseq * N_GROUPS, GROUP_W, GROUP_W), _F32)],
        compiler_params=pltpu.CompilerParams(
            dimension_semantics=("arbitrary", "arbitrary"),
            vmem_limit_bytes=VMEM_LIMIT_BYTES),
        name="layer",
    )(x2d, pe2d, s0, shift0, conv0, *vecs, *mats)


def _group_heads(wkv):
    b = wkv.shape[0]
    w = wkv.reshape(b, N_GROUPS, HEADS_PER_GROUP, HEAD_DIM, HEAD_DIM)
    return jnp.transpose(w, (0, 1, 3, 2, 4)).reshape(b, N_GROUPS, HEAD_DIM, GROUP_W)


def _ungroup_heads(s):
    b = s.shape[0]
    w = s.reshape(b, N_GROUPS, HEAD_DIM, HEADS_PER_GROUP, HEAD_DIM)
    return jnp.transpose(w, (0, 1, 3, 2, 4)).reshape(b, N_HEADS, HEAD_DIM, HEAD_DIM)


def _layer(x, pe, wkv0, shift0, conv0, vecs, mats, ts, nseq):
    b, t, _ = x.shape
    n_valid = ts
    if t < ts:
        n_valid = t
        x = jnp.pad(x, ((0, 0), (0, ts - t), (0, 0)))
        pe = jnp.pad(pe, ((0, 0), (0, ts - t), (0, 0)))
    tp = x.shape[1]
    y, s, sh, cvs = _layer_call(x.reshape(b * tp, D_MODEL), pe.reshape(b * tp, PLE_DIM),
                                _group_heads(wkv0.astype(_F32)), shift0.reshape(b, 1, SHIFT_COLS), conv0,
                                vecs, mats, nseq, ts, tp // ts, n_valid)
    return (y.reshape(b, tp, D_MODEL)[:, :t], _ungroup_heads(s), sh.reshape(b, SHIFT_COLS), cvs)


def kernel(x_prompt, x_sample, state_wkv, state_shift, state_conv, p_prompt, p_sample, g_norm, w_in, mu_shift, w_decay0, w_decay2, w_iclr0, w_iclr2, k_removal, k_replace, r_bonus, gn_w, gn_b, conv_w, w_o_a, w_o_b, w_out, g_ple, w_ple_gate, w_ple, g_final):
    assert g_norm.shape[0] == 1, "single-layer trunk"
    bp = x_prompt.shape[0]
    row = lambda a: a.reshape(1, -1).astype(_F32)
    zeros = jnp.zeros((LORA, WIDTH_A), _F32)
    w_lora = jnp.concatenate(
        [jnp.concatenate([w_decay2[0], zeros], axis=1),
         jnp.concatenate([zeros, w_iclr2[0]], axis=1)], axis=0).astype(_BF16)
    vecs = [row(g_norm[0]), row(mu_shift[0]), row(w_decay0[0]), row(w_iclr0[0]), row(k_removal[0]),
            row(k_replace[0]), row(r_bonus[0]), row(gn_w[0]), row(gn_b[0]), conv_w[0].astype(_F32),
            row(g_ple[0]), row(g_final)]
    mats = [w_in[0].astype(_BF16), w_lora, w_o_a[0].astype(_BF16), w_o_b[0].astype(_BF16),
            w_out[0].astype(_BF16), w_ple_gate[0].astype(_BF16), w_ple[0].astype(_BF16)]

    zero_wkv = jnp.zeros((bp, N_HEADS, HEAD_DIM, HEAD_DIM), _F32)
    zero_shift = jnp.zeros((bp, SHIFT_COLS), _F32)
    zero_conv = jnp.zeros((bp, CONV_W - 1, WIDTH_B), _F32)
    yp, wkv_p, shift_p, conv_p = _layer(
        x_prompt, p_prompt[0], zero_wkv, zero_shift, zero_conv, vecs, mats, ts=PROMPT_TILE, nseq=1)
    ys, wkv_s, shift_s, conv_s = _layer(
        x_sample, p_sample[0], state_wkv[0], state_shift[0], state_conv[0], vecs, mats, ts=CHUNK,
        nseq=SAMPLE_SEQS_PER_STEP)
    dt = x_prompt.dtype
    return (yp, ys,
            wkv_p.astype(dt)[None], shift_p[None], conv_p[None],
            wkv_s.astype(dt)[None], shift_s[None], conv_s[None])
```

```python
import functools
import math

import jax
import jax.numpy as jnp
from jax import lax
from jax.experimental import pallas as pl
from jax.experimental.pallas import tpu as pltpu

D_MODEL = 1024
HEAD_DIM = 64
N_HEADS = 16
WIDTH_A = N_HEADS * HEAD_DIM
LORA = 64
WIDTH_B = 1024
PLE_DIM = 256
CONV_W = 3
EPS = 1e-6
GN_EPS = 64e-5
DECAY_SCALE = math.exp(-0.5)
KAPPA_NORM_FLOOR = 1e-12
SHIFT_COLS = 3 * WIDTH_A + 2 * LORA + WIDTH_A
CONV_COLS = 4 * WIDTH_B
GATE_COLS = 2 * D_MODEL
IN_COLS = SHIFT_COLS + CONV_COLS + GATE_COLS

MXU_TILE_V7X = 256
CHUNK = 64
GROUP_W = MXU_TILE_V7X
HEADS_PER_GROUP = GROUP_W // HEAD_DIM
HEAD_SHIFT = HEAD_DIM.bit_length() - 1
assert 1 << HEAD_SHIFT == HEAD_DIM and CHUNK == HEAD_DIM
N_GROUPS = WIDTH_A // GROUP_W
LR_COL = 3 * WIDTH_A
ZA_COL = LR_COL + 2 * LORA
PAD_TOP = 8
PROMPT_TILE = 256
SAMPLE_SEQS_PER_STEP = 4
VMEM_LIMIT_BYTES = 60 * 1024 * 1024
N_STAGE_SLOTS = 2
PREP_W = 2 * MXU_TILE_V7X

_F32 = jnp.float32
_BF16 = jnp.bfloat16


def _sigmoid(x):
    return 0.5 * jnp.tanh(0.5 * x) + 0.5


def _dot(a, b):
    return jnp.dot(a, b, preferred_element_type=_F32)


def _dot_nt(a, b):
    return lax.dot_general(a, b, (((1,), (1,)), ((), ())), preferred_element_type=_F32)


def _dot_tn(a, b):
    return lax.dot_general(a, b, (((0,), (0,)), ((), ())), preferred_element_type=_F32)


def _rmsnorm(x, g):
    return x * lax.rsqrt(jnp.mean(x * x, axis=-1, keepdims=True) + EPS) * g


def _block_diag_stack(x, head_of_lane):
    parts = [jnp.where(head_of_lane == h, x, 0.0) for h in range(HEADS_PER_GROUP)]
    return jnp.concatenate(parts, axis=0)


def _scan_stages(bufs, s_ref, si, g, r0):
    c = CHUNK
    rbuf, vbuf, lwbuf, khbuf, pbuf, ktbuf = bufs
    rows = slice(r0, r0 + c)
    cs = slice(g * GROUP_W, (g + 1) * GROUP_W)

    row = lax.broadcasted_iota(jnp.int32, (c, GROUP_W), 0)
    lane = lax.broadcasted_iota(jnp.int32, (c, GROUP_W), 1)
    pos = lane & (HEAD_DIM - 1)
    head = lane >> HEAD_SHIFT
    strict = pos < row
    incl = pos <= row
    eye = pos == row
    r2 = lax.broadcasted_iota(jnp.int32, (GROUP_W, GROUP_W), 0)
    l2 = lax.broadcasted_iota(jnp.int32, (GROUP_W, GROUP_W), 1)
    same_head = (r2 >> HEAD_SHIFT) == (l2 >> HEAD_SHIFT)
    bd = functools.partial(_block_diag_stack, head_of_lane=head)

    lw = lwbuf[rows, cs]
    tr = lax.broadcasted_iota(jnp.int32, (c, c), 0)
    tc_ = lax.broadcasted_iota(jnp.int32, (c, c), 1)
    tri = (tc_ <= tr).astype(_BF16)
    hi = lw.astype(_BF16)
    lo = (lw - hi.astype(_F32)).astype(_BF16)
    cl2 = _dot(tri, jnp.concatenate([hi, lo], axis=1))
    yield

    cl = cl2[:, :GROUP_W] + cl2[:, GROUP_W:]
    g_end = jnp.exp(cl[c - 1:c, :])
    e_inv = jnp.exp(-cl)
    v = vbuf[rows, cs]
    rd = rbuf[rows, cs] * jnp.exp(cl)
    khd = khbuf[rows, cs] * jnp.exp(cl - lw)
    pd = pbuf[rows, cs] * e_inv
    ktd = ktbuf[rows, cs] * e_inv
    pdg = pd * g_end
    ktdg = ktd * g_end
    ab = _dot_nt(jnp.concatenate([khd, rd], axis=0).astype(_BF16),
                 jnp.concatenate([bd(pd), bd(ktd)], axis=0).astype(_BF16))
    yield

    b_all = jnp.where(strict, ab[:c, GROUP_W:], 0.0)
    by_all = jnp.where(incl, ab[c:, GROUP_W:], 0.0)
    ay_all = jnp.where(incl, ab[c:, :GROUP_W], 0.0)
    n = jnp.where(strict, -ab[:c, :GROUP_W], 0.0)
    t_all = jnp.where(eye, 1.0, 0.0) + n
    bv2 = _dot(jnp.concatenate([b_all, by_all], axis=0).astype(_BF16), bd(v).astype(_BF16))
    n = _dot(n.astype(_BF16), bd(n).astype(_BF16))
    yield
    for _ in range(4):
        both = _dot(jnp.concatenate([t_all, n], axis=0).astype(_BF16), bd(n).astype(_BF16))
        yield
        t_all = t_all + both[:c]
        n = both[c:]
    tn = _dot(t_all.astype(_BF16), bd(n).astype(_BF16))
    yield
    t_all = t_all + tn

    bv = bv2[:c]
    byv = bv2[c:]
    tz = _dot(t_all.astype(_BF16), jnp.concatenate([bd(khd), bd(bv)], axis=1).astype(_BF16))
    yield
    wt = tz[:, :GROUP_W]
    ut = tz[:, GROUP_W:]
    aywu = _dot(ay_all.astype(_BF16), jnp.concatenate([bd(wt), bd(ut)], axis=1).astype(_BF16))
    gmat = _dot_tn(pdg.astype(_BF16), wt.astype(_BF16))
    nt = _dot_tn(jnp.concatenate([v, ut], axis=0).astype(_BF16),
                 jnp.concatenate([ktdg, -pdg], axis=0).astype(_BF16))
    yield
    qt = (rd - aywu[:, :GROUP_W]).astype(_BF16)
    y0 = byv - aywu[:, GROUP_W:]
    gmat = jnp.where(same_head, gmat, 0.0).astype(_BF16)
    nt = jnp.where(same_head, nt, 0.0)

    s0 = s_ref[si]
    s0b = s0.astype(_BF16)
    khbuf[rows, cs] = _dot_nt(qt, s0b) + y0
    s_ref[si] = s0 * g_end - _dot_nt(s0b, gmat) + nt


def _run_lockstep(units):
    units = list(units)
    while units:
        alive = []
        for u in units:
            try:
                next(u)
                alive.append(u)
            except StopIteration:
                pass
        units = alive


def _layer_kernel(x_ref, pe_ref, s0_ref, shift0_ref, conv0_ref,
                  gn_ref, mu_ref, wd0_ref, wi0_ref, krem_ref, krep_ref, rb_ref, gnw_ref, gnb_ref,
                  cw_ref, gple_ref, gfin_ref,
                  win_ref, wlora_ref, woa_ref, wob_ref, wout_ref, wpg_ref, wple_ref,
                  y_ref, sout_ref, shout_ref, cvout_ref,
                  hbuf, stage, carry, tails, rbuf, vbuf, lwbuf, khbuf, pbuf, ktbuf, ubuf, oabuf, obbuf, mbuf, s_ref,
                  *, nseq, ts, n_valid):
    tc = nseq * ts
    j = pl.program_id(1)
    last = pl.num_programs(1) - 1
    c = CHUNK
    head_of_lane = lax.broadcasted_iota(jnp.int32, (HEAD_DIM, GROUP_W), 1) >> HEAD_SHIFT

    @pl.when(j == 0)
    def _():
        for q in range(nseq):
            for g in range(N_GROUPS):
                s_ref[q * N_GROUPS + g] = _block_diag_stack(s0_ref[q, g], head_of_lane)
            carry[q:q + 1, :] = shift0_ref[q]
            tails[q] = conv0_ref[q]

    hbuf[...] = _rmsnorm(x_ref[...], gn_ref[...]).astype(_BF16)

    stage_slot = [0]

    def shifted_mix(col, width):
        cols = slice(col, col + width)
        st = stage.at[stage_slot[0]]
        stage_slot[0] = (stage_slot[0] + 1) % N_STAGE_SLOTS
        pa = _dot(hbuf[...], win_ref[:, cols])
        st[PAD_TOP:PAD_TOP + tc, :width] = pa
        new_last = [st[PAD_TOP + q * ts + ts - 1:PAD_TOP + q * ts + ts, :width] for q in range(nseq)]
        for q in range(nseq):
            shout_ref[q, :, cols] = st[PAD_TOP + q * ts + n_valid - 1:PAD_TOP + q * ts + n_valid, :width]
        for q in range(nseq):
            st[PAD_TOP + q * ts - 1:PAD_TOP + q * ts, :width] = carry[q:q + 1, cols]
        prv = st[PAD_TOP - 1:PAD_TOP - 1 + tc, :width]
        for q in range(nseq):
            carry[q:q + 1, cols] = new_last[q]
        return pa + mu_ref[:, cols] * (prv - pa)

    r2 = lax.broadcasted_iota(jnp.int32, (GROUP_W, GROUP_W), 0)
    l2 = lax.broadcasted_iota(jnp.int32, (GROUP_W, GROUP_W), 1)
    ones_bd = ((r2 >> HEAD_SHIFT) == (l2 >> HEAD_SHIFT)).astype(_BF16)

    def head_sum(x):
        return _dot(x.astype(_BF16), ones_bd)

    lr = shifted_mix(LR_COL, 2 * LORA)
    lr_lane = lax.broadcasted_iota(jnp.int32, lr.shape, 1)
    lr_act = jnp.where(lr_lane < LORA, jnp.tanh(lr), lr).astype(_BF16)
    valid = None
    if n_valid < ts:
        valid = lax.rem(lax.broadcasted_iota(jnp.int32, (tc, PREP_W), 0), ts) < n_valid
    for c0 in range(0, WIDTH_A, PREP_W):
        cs = slice(c0, c0 + PREP_W)
        rbuf[:, cs] = shifted_mix(c0, PREP_W)
        vbuf[:, cs] = shifted_mix(2 * WIDTH_A + c0, PREP_W)
        k = shifted_mix(WIDTH_A + c0, PREP_W)
        lw = -DECAY_SCALE * _sigmoid(wd0_ref[:, cs] + _dot(lr_act, wlora_ref[:, cs]))
        a = _sigmoid(wi0_ref[:, cs] + _dot(lr_act, wlora_ref[:, WIDTH_A + c0:WIDTH_A + c0 + PREP_W]))
        kappa = k * krem_ref[:, cs]
        ksq = kappa * kappa
        n2 = jnp.concatenate([head_sum(ksq[:, o:o + GROUP_W]) for o in range(0, PREP_W, GROUP_W)], axis=1)
        kh = kappa * lax.rsqrt(jnp.maximum(n2, KAPPA_NORM_FLOOR ** 2))
        kt = k * (1.0 + (a - 1.0) * krep_ref[:, cs])
        p = a * kh
        if valid is not None:
            lw = jnp.where(valid, lw, 0.0)
            p = jnp.where(valid, p, 0.0)
            kt = jnp.where(valid, kt, 0.0)
        lwbuf[:, cs] = lw
        khbuf[:, cs] = kh
        pbuf[:, cs] = p
        ktbuf[:, cs] = kt

    bufs = (rbuf, vbuf, lwbuf, khbuf, pbuf, ktbuf)
    _run_lockstep([_scan_stages(bufs, s_ref, q * N_GROUPS + g, g, q * ts + ci * c)
                   for q in range(nseq) for ci in range(ts // c) for g in range(N_GROUPS)])

    @pl.when(j == last)
    def _():
        for q in range(nseq):
            for g in range(N_GROUPS):
                s = s_ref[q * N_GROUPS + g]
                sout_ref[q, g] = (s[0:HEAD_DIM] + s[HEAD_DIM:2 * HEAD_DIM]
                                  + s[2 * HEAD_DIM:3 * HEAD_DIM] + s[3 * HEAD_DIM:4 * HEAD_DIM])

    for g in range(N_GROUPS):
        c0 = g * GROUP_W
        cs = slice(c0, c0 + GROUP_W)
        y = khbuf[:, cs]
        yc = y - head_sum(y) * (1.0 / HEAD_DIM)
        var = head_sum(yc * yc) * (1.0 / HEAD_DIM)
        yn = yc * lax.rsqrt(var + GN_EPS) * gnw_ref[:, cs] + gnb_ref[:, cs]
        bonus = head_sum(rbuf[:, cs] * ktbuf[:, cs] * rb_ref[:, cs]) * vbuf[:, cs]
        za = shifted_mix(ZA_COL + c0, GROUP_W)
        oabuf[:, cs] = ((yn + bonus) * (za * _sigmoid(za))).astype(_BF16)

    def proj_cols(col):
        return _dot(hbuf[...], win_ref[:, col:col + GROUP_W])

    for blk in range(0, WIDTH_B, GROUP_W):
        bs = slice(blk, blk + GROUP_W)
        gb = proj_cols(SHIFT_COLS + blk)
        gc = proj_cols(SHIFT_COLS + WIDTH_B + blk)
        xb = proj_cols(SHIFT_COLS + 2 * WIDTH_B + blk)
        zb = proj_cols(SHIFT_COLS + 3 * WIDTH_B + blk)
        u = gc * xb
        ubuf[PAD_TOP:PAD_TOP + tc, bs] = u
        new_tail = [ubuf[PAD_TOP + q * ts + ts - 2:PAD_TOP + q * ts + ts, bs] for q in range(nseq)]
        for q in range(nseq):
            cvout_ref[q, :, bs] = ubuf[PAD_TOP + q * ts + n_valid - 2:PAD_TOP + q * ts + n_valid, bs]
        for q in range(nseq):
            ubuf[PAD_TOP + q * ts - 2:PAD_TOP + q * ts, bs] = tails[q, :, bs]
        cv = (cw_ref[0:1, bs] * ubuf[PAD_TOP - 2:PAD_TOP - 2 + tc, bs]
              + cw_ref[1:2, bs] * ubuf[PAD_TOP - 1:PAD_TOP - 1 + tc, bs]
              + cw_ref[2:3, bs] * u)
        for q in range(nseq):
            tails[q, :, bs] = new_tail[q]
        obbuf[:, bs] = (gb * cv * (zb * _sigmoid(zb))).astype(_BF16)

    g0 = SHIFT_COLS + CONV_COLS
    for blk in range(0, D_MODEL, GROUP_W):
        bs = slice(blk, blk + GROUP_W)
        out_a = _dot(oabuf[...], woa_ref[:, bs])
        out_b = _dot(obbuf[...], wob_ref[:, bs])
        ga = _sigmoid(proj_cols(g0 + blk))
        gbm = _sigmoid(proj_cols(g0 + D_MODEL + blk))
        mbuf[:, bs] = (ga * out_a + gbm * out_b).astype(_BF16)

    x1 = x_ref[...] + _dot(mbuf[...], wout_ref[...])
    hp = _rmsnorm(x1, gple_ref[...]).astype(_BF16)
    gate = _sigmoid(_dot(hp, wpg_ref[...]))
    x2 = x1 + gate * _dot(pe_ref[...].astype(_BF16), wple_ref[...])
    y_ref[...] = _rmsnorm(x2, gfin_ref[...])


def _layer_call(x2d, pe2d, s0, shift0, conv0, vecs, mats, nseq, ts, nt, n_valid):
    b = s0.shape[0]
    tc = nseq * ts
    assert nseq == 1 or (nt == 1 and n_valid <= ts - (CONV_W - 1))
    assert n_valid == ts or nt == 1

    def tile(width):
        return pl.BlockSpec((tc, width), lambda bi, ji: (bi * nt + ji, 0))

    def per_seq(shape):
        return pl.BlockSpec((nseq,) + shape, lambda bi, ji: (bi,) + (0,) * len(shape))

    def resident(arr):
        return pl.BlockSpec(arr.shape, lambda bi, ji: (0,) * arr.ndim, pipeline_mode=pl.Buffered(1))

    state_shape = (N_GROUPS, HEAD_DIM, GROUP_W)
    wide = lambda dtype: pltpu.VMEM((tc, WIDTH_A), dtype)
    kern = functools.partial(_layer_kernel, nseq=nseq, ts=ts, n_valid=n_valid)
    return pl.pallas_call(
        kern,
        grid=(b // nseq, nt),
        in_specs=[tile(D_MODEL), tile(PLE_DIM),
                  per_seq(state_shape), per_seq((1, SHIFT_COLS)), per_seq((CONV_W - 1, WIDTH_B))]
                 + [resident(a) for a in vecs] + [resident(a) for a in mats],
        out_specs=[tile(D_MODEL), per_seq(state_shape), per_seq((1, SHIFT_COLS)),
                   per_seq((CONV_W - 1, WIDTH_B))],
        out_shape=[jax.ShapeDtypeStruct(x2d.shape, _F32),
                   jax.ShapeDtypeStruct((b,) + state_shape, _F32),
                   jax.ShapeDtypeStruct((b, 1, SHIFT_COLS), _F32),
                   jax.ShapeDtypeStruct((b, CONV_W - 1, WIDTH_B), _F32)],
        scratch_shapes=[pltpu.VMEM((tc, D_MODEL), _BF16),
                        pltpu.VMEM((N_STAGE_SLOTS, PAD_TOP + tc, PREP_W), _F32),
                        pltpu.VMEM((nseq, SHIFT_COLS), _F32),
                        pltpu.VMEM((nseq, CONV_W - 1, WIDTH_B), _F32),
                        wide(_F32), wide(_F32), wide(_F32),
                        wide(_F32), wide(_F32), wide(_F32),
                        pltpu.VMEM((PAD_TOP + tc, WIDTH_B), _F32),
                        wide(_BF16), wide(_BF16), wide(_BF16),
                        pltpu.VMEM((nseq * N_GROUPS, GROUP_W, GROUP_W), _F32)],
        compiler_params=pltpu.CompilerParams(
            dimension_semantics=("arbitrary", "arbitrary"),
            vmem_limit_bytes=VMEM_LIMIT_BYTES),
        name="layer",
    )(x2d, pe2d, s0, shift0, conv0, *vecs, *mats)


def _group_heads(wkv):
    b = wkv.shape[0]
    w = wkv.reshape(b, N_GROUPS, HEADS_PER_GROUP, HEAD_DIM, HEAD_DIM)
    return jnp.transpose(w, (0, 1, 3, 2, 4)).reshape(b, N_GROUPS, HEAD_DIM, GROUP_W)


def _ungroup_heads(s):
    b = s.shape[0]
    w = s.reshape(b, N_GROUPS, HEAD_DIM, HEADS_PER_GROUP, HEAD_DIM)
    return jnp.transpose(w, (0, 1, 3, 2, 4)).reshape(b, N_HEADS, HEAD_DIM, HEAD_DIM)


def _layer(x, pe, wkv0, shift0, conv0, vecs, mats, ts, nseq):
    b, t, _ = x.shape
    n_valid = ts
    if t < ts:
        n_valid = t
        x = jnp.pad(x, ((0, 0), (0, ts - t), (0, 0)))
        pe = jnp.pad(pe, ((0, 0), (0, ts - t), (0, 0)))
    tp = x.shape[1]
    y, s, sh, cvs = _layer_call(x.reshape(b * tp, D_MODEL), pe.reshape(b * tp, PLE_DIM),
                                _group_heads(wkv0.astype(_F32)), shift0.reshape(b, 1, SHIFT_COLS), conv0,
                                vecs, mats, nseq, ts, tp // ts, n_valid)
    return (y.reshape(b, tp, D_MODEL)[:, :t], _ungroup_heads(s), sh.reshape(b, SHIFT_COLS), cvs)


def kernel(x_prompt, x_sample, state_wkv, state_shift, state_conv, p_prompt, p_sample, g_norm, w_in, mu_shift, w_decay0, w_decay2, w_iclr0, w_iclr2, k_removal, k_replace, r_bonus, gn_w, gn_b, conv_w, w_o_a, w_o_b, w_out, g_ple, w_ple_gate, w_ple, g_final):
    assert g_norm.shape[0] == 1, "single-layer trunk"
    assert w_in.shape[1:] == (D_MODEL, IN_COLS)
    bp = x_prompt.shape[0]
    row = lambda a: a.reshape(1, -1).astype(_F32)
    zeros = jnp.zeros((LORA, WIDTH_A), _F32)
    w_lora = jnp.concatenate(
        [jnp.concatenate([w_decay2[0], zeros], axis=1),
         jnp.concatenate([zeros, w_iclr2[0]], axis=1)], axis=0).astype(_BF16)
    vecs = [row(g_norm[0]), row(mu_shift[0]), row(w_decay0[0]), row(w_iclr0[0]), row(k_removal[0]),
            row(k_replace[0]), row(r_bonus[0]), row(gn_w[0]), row(gn_b[0]), conv_w[0].astype(_F32),
            row(g_ple[0]), row(g_final)]
    mats = [w_in[0].astype(_BF16), w_lora, w_o_a[0].astype(_BF16), w_o_b[0].astype(_BF16),
            w_out[0].astype(_BF16), w_ple_gate[0].astype(_BF16), w_ple[0].astype(_BF16)]

    zero_wkv = jnp.zeros((bp, N_HEADS, HEAD_DIM, HEAD_DIM), _F32)
    zero_shift = jnp.zeros((bp, SHIFT_COLS), _F32)
    zero_conv = jnp.zeros((bp, CONV_W - 1, WIDTH_B), _F32)
    yp, wkv_p, shift_p, conv_p = _layer(
        x_prompt, p_prompt[0], zero_wkv, zero_shift, zero_conv, vecs, mats, ts=PROMPT_TILE, nseq=1)
    ys, wkv_s, shift_s, conv_s = _layer(
        x_sample, p_sample[0], state_wkv[0], state_shift[0], state_conv[0], vecs, mats, ts=CHUNK,
        nseq=SAMPLE_SEQS_PER_STEP)
    dt = x_prompt.dtype
    return (yp, ys,
            wkv_p.astype(dt)[None], shift_p[None], conv_p[None],
            wkv_s.astype(dt)[None], shift_s[None], conv_s[None])
```

```python
import functools
import math

import jax
import jax.numpy as jnp
from jax import lax
from jax.experimental import pallas as pl
from jax.experimental.pallas import tpu as pltpu

D_MODEL = 1024
HEAD_DIM = 64
N_HEADS = 16
WIDTH_A = N_HEADS * HEAD_DIM
LORA = 64
WIDTH_B = 1024
PLE_DIM = 256
CONV_W = 3
EPS = 1e-6
GN_EPS = 64e-5
DECAY_SCALE = math.exp(-0.5)
KAPPA_NORM_FLOOR = 1e-12
SHIFT_COLS = 3 * WIDTH_A + 2 * LORA + WIDTH_A
CONV_COLS = 4 * WIDTH_B
GATE_COLS = 2 * D_MODEL
IN_COLS = SHIFT_COLS + CONV_COLS + GATE_COLS

MXU_TILE_V7X = 256
CHUNK = 64
GROUP_W = MXU_TILE_V7X
HEADS_PER_GROUP = GROUP_W // HEAD_DIM
HEAD_SHIFT = HEAD_DIM.bit_length() - 1
assert 1 << HEAD_SHIFT == HEAD_DIM and CHUNK == HEAD_DIM
N_GROUPS = WIDTH_A // GROUP_W
LR_COL = 3 * WIDTH_A
ZA_COL = LR_COL + 2 * LORA
PAD_TOP = 8
PROMPT_TILE = 256
SAMPLE_SEQS_PER_STEP = 4
VMEM_LIMIT_BYTES = 60 * 1024 * 1024
N_STAGE_SLOTS = 2
PREP_W = 2 * MXU_TILE_V7X

_F32 = jnp.float32
_BF16 = jnp.bfloat16


def _sigmoid(x):
    return 0.5 * jnp.tanh(0.5 * x) + 0.5


def _dot(a, b):
    return jnp.dot(a, b, preferred_element_type=_F32)


def _dot_nt(a, b):
    return lax.dot_general(a, b, (((1,), (1,)), ((), ())), preferred_element_type=_F32)


def _dot_tn(a, b):
    return lax.dot_general(a, b, (((0,), (0,)), ((), ())), preferred_element_type=_F32)


def _rmsnorm(x, g):
    return x * lax.rsqrt(jnp.mean(x * x, axis=-1, keepdims=True) + EPS) * g


def _block_diag_stack(x, head_of_lane):
    parts = [jnp.where(head_of_lane == h, x, 0.0) for h in range(HEADS_PER_GROUP)]
    return jnp.concatenate(parts, axis=0)


def _scan_stages(bufs, s_ref, si, g, r0):
    c = CHUNK
    rbuf, vbuf, lwbuf, khbuf, pbuf, ktbuf = bufs
    rows = slice(r0, r0 + c)
    cs = slice(g * GROUP_W, (g + 1) * GROUP_W)

    row = lax.broadcasted_iota(jnp.int32, (c, GROUP_W), 0)
    lane = lax.broadcasted_iota(jnp.int32, (c, GROUP_W), 1)
    pos = lane & (HEAD_DIM - 1)
    head = lane >> HEAD_SHIFT
    strict = pos < row
    incl = pos <= row
    eye = pos == row
    r2 = lax.broadcasted_iota(jnp.int32, (GROUP_W, GROUP_W), 0)
    l2 = lax.broadcasted_iota(jnp.int32, (GROUP_W, GROUP_W), 1)
    same_head = (r2 >> HEAD_SHIFT) == (l2 >> HEAD_SHIFT)
    bd = functools.partial(_block_diag_stack, head_of_lane=head)

    lw = lwbuf[rows, cs]
    tr = lax.broadcasted_iota(jnp.int32, (c, c), 0)
    tc_ = lax.broadcasted_iota(jnp.int32, (c, c), 1)
    tri = (tc_ <= tr).astype(_BF16)
    hi = lw.astype(_BF16)
    lo = (lw - hi.astype(_F32)).astype(_BF16)
    cl2 = _dot(tri, jnp.concatenate([hi, lo], axis=1))
    yield

    cl = cl2[:, :GROUP_W] + cl2[:, GROUP_W:]
    g_end = jnp.exp(cl[c - 1:c, :])
    e_inv = jnp.exp(-cl)
    v = vbuf[rows, cs]
    rd = rbuf[rows, cs] * jnp.exp(cl)
    khd = khbuf[rows, cs] * jnp.exp(cl - lw)
    pd = pbuf[rows, cs] * e_inv
    ktd = ktbuf[rows, cs] * e_inv
    pdg = pd * g_end
    ktdg = ktd * g_end
    ab = _dot_nt(jnp.concatenate([khd, rd], axis=0).astype(_BF16),
                 jnp.concatenate([bd(pd), bd(ktd)], axis=0).astype(_BF16))
    yield

    b_all = jnp.where(strict, ab[:c, GROUP_W:], 0.0)
    by_all = jnp.where(incl, ab[c:, GROUP_W:], 0.0)
    ay_all = jnp.where(incl, ab[c:, :GROUP_W], 0.0)
    n = jnp.where(strict, -ab[:c, :GROUP_W], 0.0)
    t_all = jnp.where(eye, 1.0, 0.0) + n
    bv2 = _dot(jnp.concatenate([b_all, by_all], axis=0).astype(_BF16), bd(v).astype(_BF16))
    n = _dot(n.astype(_BF16), bd(n).astype(_BF16))
    yield
    for _ in range(4):
        both = _dot(jnp.concatenate([t_all, n], axis=0).astype(_BF16), bd(n).astype(_BF16))
        yield
        t_all = t_all + both[:c]
        n = both[c:]
    tn = _dot(t_all.astype(_BF16), bd(n).astype(_BF16))
    yield
    t_all = t_all + tn

    bv = bv2[:c]
    byv = bv2[c:]
    tz = _dot(t_all.astype(_BF16), jnp.concatenate([bd(khd), bd(bv)], axis=1).astype(_BF16))
    yield
    wt = tz[:, :GROUP_W]
    ut = tz[:, GROUP_W:]
    aywu = _dot(ay_all.astype(_BF16), jnp.concatenate([bd(wt), bd(ut)], axis=1).astype(_BF16))
    gmat = _dot_tn(pdg.astype(_BF16), wt.astype(_BF16))
    nt = _dot_tn(jnp.concatenate([v, ut], axis=0).astype(_BF16),
                 jnp.concatenate([ktdg, -pdg], axis=0).astype(_BF16))
    yield
    qt = (rd - aywu[:, :GROUP_W]).astype(_BF16)
    y0 = byv - aywu[:, GROUP_W:]
    gmat = jnp.where(same_head, gmat, 0.0).astype(_BF16)
    nt = jnp.where(same_head, nt, 0.0)

    s0 = s_ref[si]
    s0b = s0.astype(_BF16)
    khbuf[rows, cs] = _dot_nt(qt, s0b) + y0
    s_ref[si] = s0 * g_end - _dot_nt(s0b, gmat) + nt


def _run_lockstep(units):
    units = list(units)
    while units:
        alive = []
        for u in units:
            try:
                next(u)
                alive.append(u)
            except StopIteration:
                pass
        units = alive


def _layer_kernel(x_ref, pe_ref, s0_ref, shift0_ref, conv0_ref,
                  gn_ref, mu_ref, wd0_ref, wi0_ref, krem_ref, krep_ref, rb_ref, gnw_ref, gnb_ref,
                  cw_ref, gple_ref, gfin_ref,
                  win_ref, wlora_ref, woa_ref, wob_ref, wout_ref, wpg_ref, wple_ref,
                  y_ref, sout_ref, shout_ref, cvout_ref,
                  hbuf, stage, carry, tails, rbuf, vbuf, lwbuf, khbuf, pbuf, ktbuf, ubuf, oabuf, obbuf, mbuf, s_ref,
                  *, nseq, ts, n_valid):
    tc = nseq * ts
    j = pl.program_id(1)
    last = pl.num_programs(1) - 1
    c = CHUNK
    head_of_lane = lax.broadcasted_iota(jnp.int32, (HEAD_DIM, GROUP_W), 1) >> HEAD_SHIFT

    @pl.when(j == 0)
    def _():
        for q in range(nseq):
            for g in range(N_GROUPS):
                s_ref[q * N_GROUPS + g] = _block_diag_stack(s0_ref[q, g], head_of_lane)
            carry[q:q + 1, :] = shift0_ref[q]
            tails[q] = conv0_ref[q]

    hbuf[...] = _rmsnorm(x_ref[...], gn_ref[...]).astype(_BF16)

    stage_slot = [0]

    def shifted_mix(col, width):
        cols = slice(col, col + width)
        st = stage.at[stage_slot[0]]
        stage_slot[0] = (stage_slot[0] + 1) % N_STAGE_SLOTS
        pa = _dot(hbuf[...], win_ref[:, cols])
        st[PAD_TOP:PAD_TOP + tc, :width] = pa
        for q in range(nseq):
            st[PAD_TOP + q * ts - 1:PAD_TOP + q * ts, :width] = carry[q:q + 1, cols]
        prv = st[PAD_TOP - 1:PAD_TOP - 1 + tc, :width]
        for q in range(nseq):
            shout_ref[q, :, cols] = pa[q * ts + n_valid - 1:q * ts + n_valid]
            carry[q:q + 1, cols] = pa[q * ts + ts - 1:q * ts + ts]
        return pa + mu_ref[:, cols] * (prv - pa)

    r2 = lax.broadcasted_iota(jnp.int32, (GROUP_W, GROUP_W), 0)
    l2 = lax.broadcasted_iota(jnp.int32, (GROUP_W, GROUP_W), 1)
    ones_bd = ((r2 >> HEAD_SHIFT) == (l2 >> HEAD_SHIFT)).astype(_BF16)

    def head_sum(x):
        return _dot(x.astype(_BF16), ones_bd)

    lr = shifted_mix(LR_COL, 2 * LORA)
    lr_lane = lax.broadcasted_iota(jnp.int32, lr.shape, 1)
    lr_act = jnp.where(lr_lane < LORA, jnp.tanh(lr), lr).astype(_BF16)
    valid = None
    if n_valid < ts:
        valid = lax.rem(lax.broadcasted_iota(jnp.int32, (tc, PREP_W), 0), ts) < n_valid
    for c0 in range(0, WIDTH_A, PREP_W):
        cs = slice(c0, c0 + PREP_W)
        rbuf[:, cs] = shifted_mix(c0, PREP_W)
        vbuf[:, cs] = shifted_mix(2 * WIDTH_A + c0, PREP_W)
        k = shifted_mix(WIDTH_A + c0, PREP_W)
        lw = -DECAY_SCALE * _sigmoid(wd0_ref[:, cs] + _dot(lr_act, wlora_ref[:, cs]))
        a = _sigmoid(wi0_ref[:, cs] + _dot(lr_act, wlora_ref[:, WIDTH_A + c0:WIDTH_A + c0 + PREP_W]))
        kappa = k * krem_ref[:, cs]
        ksq = kappa * kappa
        n2 = jnp.concatenate([head_sum(ksq[:, o:o + GROUP_W]) for o in range(0, PREP_W, GROUP_W)], axis=1)
        kh = kappa * lax.rsqrt(jnp.maximum(n2, KAPPA_NORM_FLOOR ** 2))
        kt = k * (1.0 + (a - 1.0) * krep_ref[:, cs])
        p = a * kh
        if valid is not None:
            lw = jnp.where(valid, lw, 0.0)
            p = jnp.where(valid, p, 0.0)
            kt = jnp.where(valid, kt, 0.0)
        lwbuf[:, cs] = lw
        khbuf[:, cs] = kh
        pbuf[:, cs] = p
        ktbuf[:, cs] = kt

    bufs = (rbuf, vbuf, lwbuf, khbuf, pbuf, ktbuf)
    _run_lockstep([_scan_stages(bufs, s_ref, q * N_GROUPS + g, g, q * ts + ci * c)
                   for q in range(nseq) for ci in range(ts // c) for g in range(N_GROUPS)])

    @pl.when(j == last)
    def _():
        for q in range(nseq):
            for g in range(N_GROUPS):
                s = s_ref[q * N_GROUPS + g]
                sout_ref[q, g] = (s[0:HEAD_DIM] + s[HEAD_DIM:2 * HEAD_DIM]
                                  + s[2 * HEAD_DIM:3 * HEAD_DIM] + s[3 * HEAD_DIM:4 * HEAD_DIM])

    for g in range(N_GROUPS):
        c0 = g * GROUP_W
        cs = slice(c0, c0 + GROUP_W)
        y = khbuf[:, cs]
        yc = y - head_sum(y) * (1.0 / HEAD_DIM)
        var = head_sum(yc * yc) * (1.0 / HEAD_DIM)
        yn = yc * lax.rsqrt(var + GN_EPS) * gnw_ref[:, cs] + gnb_ref[:, cs]
        bonus = head_sum(rbuf[:, cs] * ktbuf[:, cs] * rb_ref[:, cs]) * vbuf[:, cs]
        za = shifted_mix(ZA_COL + c0, GROUP_W)
        oabuf[:, cs] = ((yn + bonus) * (za * _sigmoid(za))).astype(_BF16)

    def proj_cols(col):
        return _dot(hbuf[...], win_ref[:, col:col + GROUP_W])

    for blk in range(0, WIDTH_B, GROUP_W):
        bs = slice(blk, blk + GROUP_W)
        gb = proj_cols(SHIFT_COLS + blk)
        gc = proj_cols(SHIFT_COLS + WIDTH_B + blk)
        xb = proj_cols(SHIFT_COLS + 2 * WIDTH_B + blk)
        zb = proj_cols(SHIFT_COLS + 3 * WIDTH_B + blk)
        u = gc * xb
        ubuf[PAD_TOP:PAD_TOP + tc, bs] = u
        for q in range(nseq):
            ubuf[PAD_TOP + q * ts - 2:PAD_TOP + q * ts, bs] = tails[q, :, bs]
        cv = (cw_ref[0:1, bs] * ubuf[PAD_TOP - 2:PAD_TOP - 2 + tc, bs]
              + cw_ref[1:2, bs] * ubuf[PAD_TOP - 1:PAD_TOP - 1 + tc, bs]
              + cw_ref[2:3, bs] * u)
        for q in range(nseq):
            cvout_ref[q, :, bs] = u[q * ts + n_valid - 2:q * ts + n_valid]
            tails[q, :, bs] = u[q * ts + ts - 2:q * ts + ts]
        obbuf[:, bs] = (gb * cv * (zb * _sigmoid(zb))).astype(_BF16)

    g0 = SHIFT_COLS + CONV_COLS
    for blk in range(0, D_MODEL, GROUP_W):
        bs = slice(blk, blk + GROUP_W)
        out_a = _dot(oabuf[...], woa_ref[:, bs])
        out_b = _dot(obbuf[...], wob_ref[:, bs])
        ga = _sigmoid(proj_cols(g0 + blk))
        gbm = _sigmoid(proj_cols(g0 + D_MODEL + blk))
        mbuf[:, bs] = (ga * out_a + gbm * out_b).astype(_BF16)

    x1 = x_ref[...] + _dot(mbuf[...], wout_ref[...])
    hp = _rmsnorm(x1, gple_ref[...]).astype(_BF16)
    gate = _sigmoid(_dot(hp, wpg_ref[...]))
    x2 = x1 + gate * _dot(pe_ref[...].astype(_BF16), wple_ref[...])
    y_ref[...] = _rmsnorm(x2, gfin_ref[...])


def _layer_call(x, pe, s0, shift0, conv0, vecs, mats, nseq, ts, n_valid):
    b, tp, _ = x.shape
    nt = tp // ts
    tc = nseq * ts
    assert nseq == 1 or (nt == 1 and n_valid <= ts - (CONV_W - 1))
    assert n_valid == ts or nt == 1
    if nseq > 1:
        x = x.reshape(b * tp, D_MODEL)
        pe = pe.reshape(b * tp, PLE_DIM)

    def tile(width):
        if nseq == 1:
            return pl.BlockSpec((None, ts, width), lambda bi, ji: (bi, ji, 0))
        return pl.BlockSpec((tc, width), lambda bi, ji: (bi, 0))

    def per_seq(shape):
        return pl.BlockSpec((nseq,) + shape, lambda bi, ji: (bi,) + (0,) * len(shape))

    def resident(arr):
        return pl.BlockSpec(arr.shape, lambda bi, ji: (0,) * arr.ndim, pipeline_mode=pl.Buffered(1))

    state_shape = (N_GROUPS, HEAD_DIM, GROUP_W)
    wide = lambda dtype: pltpu.VMEM((tc, WIDTH_A), dtype)
    kern = functools.partial(_layer_kernel, nseq=nseq, ts=ts, n_valid=n_valid)
    return pl.pallas_call(
        kern,
        grid=(b // nseq, nt),
        in_specs=[tile(D_MODEL), tile(PLE_DIM),
                  per_seq(state_shape), per_seq((1, SHIFT_COLS)), per_seq((CONV_W - 1, WIDTH_B))]
                 + [resident(a) for a in vecs] + [resident(a) for a in mats],
        out_specs=[tile(D_MODEL), per_seq(state_shape), per_seq((1, SHIFT_COLS)),
                   per_seq((CONV_W - 1, WIDTH_B))],
        out_shape=[jax.ShapeDtypeStruct(x.shape, _F32),
                   jax.ShapeDtypeStruct((b,) + state_shape, _F32),
                   jax.ShapeDtypeStruct((b, 1, SHIFT_COLS), _F32),
                   jax.ShapeDtypeStruct((b, CONV_W - 1, WIDTH_B), _F32)],
        scratch_shapes=[pltpu.VMEM((tc, D_MODEL), _BF16),
                        pltpu.VMEM((N_STAGE_SLOTS, PAD_TOP + tc, PREP_W), _F32),
                        pltpu.VMEM((nseq, SHIFT_COLS), _F32),
                        pltpu.VMEM((nseq, CONV_W - 1, WIDTH_B), _F32),
                        wide(_F32), wide(_F32), wide(_F32),
                        wide(_F32), wide(_F32), wide(_F32),
                        pltpu.VMEM((PAD_TOP + tc, WIDTH_B), _F32),
                        wide(_BF16), wide(_BF16), wide(_BF16),
                        pltpu.VMEM((nseq * N_GROUPS, GROUP_W, GROUP_W), _F32)],
        compiler_params=pltpu.CompilerParams(
            dimension_semantics=("arbitrary", "arbitrary"),
            vmem_limit_bytes=VMEM_LIMIT_BYTES),
        name="layer",
    )(x, pe, s0, shift0, conv0, *vecs, *mats)


def _group_heads(wkv):
    b = wkv.shape[0]
    w = wkv.reshape(b, N_GROUPS, HEADS_PER_GROUP, HEAD_DIM, HEAD_DIM)
    return jnp.transpose(w, (0, 1, 3, 2, 4)).reshape(b, N_GROUPS, HEAD_DIM, GROUP_W)


def _ungroup_heads(s):
    b = s.shape[0]
    w = s.reshape(b, N_GROUPS, HEAD_DIM, HEADS_PER_GROUP, HEAD_DIM)
    return jnp.transpose(w, (0, 1, 3, 2, 4)).reshape(b, N_HEADS, HEAD_DIM, HEAD_DIM)


def _layer(x, pe, wkv0, shift0, conv0, vecs, mats, ts, nseq):
    b, t, _ = x.shape
    n_valid = ts
    if t < ts:
        n_valid = t
        x = jnp.pad(x, ((0, 0), (0, ts - t), (0, 0)))
        pe = jnp.pad(pe, ((0, 0), (0, ts - t), (0, 0)))
    tp = x.shape[1]
    y, s, sh, cvs = _layer_call(x, pe, _group_heads(wkv0.astype(_F32)), shift0.reshape(b, 1, SHIFT_COLS), conv0,
                                vecs, mats, nseq, ts, n_valid)
    return (y.reshape(b, tp, D_MODEL)[:, :t], _ungroup_heads(s), sh.reshape(b, SHIFT_COLS), cvs)


def kernel(x_prompt, x_sample, state_wkv, state_shift, state_conv, p_prompt, p_sample, g_norm, w_in, mu_shift, w_decay0, w_decay2, w_iclr0, w_iclr2, k_removal, k_replace, r_bonus, gn_w, gn_b, conv_w, w_o_a, w_o_b, w_out, g_ple, w_ple_gate, w_ple, g_final):
    assert g_norm.shape[0] == 1, "single-layer trunk"
    assert w_in.shape[1:] == (D_MODEL, IN_COLS)
    bp = x_prompt.shape[0]
    row = lambda a: a.reshape(1, -1).astype(_F32)
    zeros = jnp.zeros((LORA, WIDTH_A), _F32)
    w_lora = jnp.concatenate(
        [jnp.concatenate([w_decay2[0], zeros], axis=1),
         jnp.concatenate([zeros, w_iclr2[0]], axis=1)], axis=0).astype(_BF16)
    vecs = [row(g_norm[0]), row(mu_shift[0]), row(w_decay0[0]), row(w_iclr0[0]), row(k_removal[0]),
            row(k_replace[0]), row(r_bonus[0]), row(gn_w[0]), row(gn_b[0]), conv_w[0].astype(_F32),
            row(g_ple[0]), row(g_final)]
    mats = [w_in[0].astype(_BF16), w_lora, w_o_a[0].astype(_BF16), w_o_b[0].astype(_BF16),
            w_out[0].astype(_BF16), w_ple_gate[0].astype(_BF16), w_ple[0].astype(_BF16)]

    zero_wkv = jnp.zeros((bp, N_HEADS, HEAD_DIM, HEAD_DIM), _F32)
    zero_shift = jnp.zeros((bp, SHIFT_COLS), _F32)
    zero_conv = jnp.zeros((bp, CONV_W - 1, WIDTH_B), _F32)
    yp, wkv_p, shift_p, conv_p = _layer(
        x_prompt, p_prompt[0], zero_wkv, zero_shift, zero_conv, vecs, mats, ts=PROMPT_TILE, nseq=1)
    ys, wkv_s, shift_s, conv_s = _layer(
        x_sample, p_sample[0], state_wkv[0], state_shift[0], state_conv[0], vecs, mats, ts=CHUNK,
        nseq=SAMPLE_SEQS_PER_STEP)
    dt = x_prompt.dtype
    return (yp, ys,
            wkv_p.astype(dt)[None], shift_p[None], conv_p[None],
            wkv_s.astype(dt)[None], shift_s[None], conv_s[None])
```
